```python
import jax, jax.numpy as jnp
from jax import lax
import numpy as np

D_MODEL = 2048
BATCH = 8
SEQ = 2048
DEPTH = 4

N_A_LAYERS = DEPTH // 2
N_B_LAYERS = DEPTH - N_A_LAYERS
HEAD_DIM = 128
A_HEADS = D_MODEL // HEAD_DIM
A_WIDTH = A_HEADS * HEAD_DIM
DILATED_GROUPS = ((128, 1), (512, 4), (2048, 16))
N_GROUPS = len(DILATED_GROUPS)
B_HEADS_PER_GROUP = 4
B_WIDTH = N_GROUPS * B_HEADS_PER_GROUP * HEAD_DIM
BAND_BLOCK = 128
QUERY_BLOCK = 128
ROT_DIM = HEAD_DIM // 4
ROPE_THETA = 500000.0
D_FF = 256 * ((8 * D_MODEL // 3 + 255) // 256)
CONV_WIDTH = 3
PLE_DIM = 256
NORM_EPS = 1e-6

kernel_name = "yoco_stickbreaking_dilated_hybrid"


def rms_norm(x, w):
    xf = x.astype(jnp.float32)
    y = xf * lax.rsqrt(jnp.mean(xf * xf, axis=-1, keepdims=True) + NORM_EPS)
    return (y * w.astype(jnp.float32)).astype(x.dtype)


def partial_rotary(x, positions):
    inv_freq = ROPE_THETA ** (-jnp.arange(0, ROT_DIM, 2, dtype=jnp.float32) / ROT_DIM)
    ang = positions.astype(jnp.float32)[..., None] * inv_freq
    ang = ang.reshape(ang.shape[:2] + (1,) * (x.ndim - 3) + ang.shape[-1:])
    cos, sin = jnp.cos(ang), jnp.sin(ang)
    xr = x[..., :ROT_DIM].astype(jnp.float32)
    x1, x2 = xr[..., : ROT_DIM // 2], xr[..., ROT_DIM // 2:]
    rot = jnp.concatenate([x1 * cos - x2 * sin, x2 * cos + x1 * sin], axis=-1).astype(x.dtype)
    return jnp.concatenate([rot, x[..., ROT_DIM:]], axis=-1)


def stick_breaking_attention(q, k, v):
    b, s, h, dh = q.shape
    n_blk = s // QUERY_BLOCK
    scale = dh ** -0.5
    q_blocks = q.reshape(b, n_blk, QUERY_BLOCK, h, dh).swapaxes(0, 1)
    key_pos = jnp.arange(s)

    def one_block(args):
        qb, blk = args
        z = jnp.einsum('bqhd,bkhd->bhqk', qb, k, preferred_element_type=jnp.float32) * scale
        q_pos = blk * QUERY_BLOCK + jnp.arange(QUERY_BLOCK)
        causal = key_pos[None, :] < q_pos[:, None]
        log_1m = jnp.where(causal, jax.nn.log_sigmoid(-z), 0.0)
        shifted = jnp.concatenate([log_1m[..., 1:], jnp.zeros_like(log_1m[..., :1])], axis=-1)
        log_stick = lax.cumsum(shifted, axis=shifted.ndim - 1, reverse=True)
        a = jnp.where(causal, jnp.exp(jax.nn.log_sigmoid(z) + log_stick), 0.0)
        return jnp.einsum('bhqk,bkhd->bqhd', a.astype(v.dtype), v)

    out = lax.map(one_block, (q_blocks, jnp.arange(n_blk)))
    return out.swapaxes(0, 1).reshape(b, s, h, dh)


def mixer_a(h, w_qkv, w_o):
    b, s, _ = h.shape
    qkv = (h @ w_qkv).reshape(b, s, 3, A_HEADS, HEAD_DIM)
    o = stick_breaking_attention(qkv[:, :, 0], qkv[:, :, 1], qkv[:, :, 2])
    return o.reshape(b, s, A_WIDTH) @ w_o


def to_dilated_blocks(x, dil):
    b, s, h, dh = x.shape
    sub_len = s // dil
    n_blk = -(-sub_len // BAND_BLOCK)
    xs = x.reshape(b, sub_len, dil, h, dh).transpose(0, 2, 1, 3, 4)
    xs = jnp.pad(xs, ((0, 0), (0, 0), (0, n_blk * BAND_BLOCK - sub_len), (0, 0), (0, 0)))
    return xs.reshape(b, dil, n_blk, BAND_BLOCK, h, dh)


def with_previous_block(xb):
    prev = jnp.pad(xb[:, :, :-1], ((0, 0), (0, 0), (1, 0), (0, 0), (0, 0), (0, 0)))
    return jnp.concatenate([prev, xb], axis=3)


def from_dilated_blocks(xb, seq):
    b, dil, n_blk, blk = xb.shape[:4]
    rest = xb.shape[4:]
    sub_len = seq // dil
    xs = xb.reshape((b, dil, n_blk * blk) + rest)[:, :, :sub_len]
    return jnp.moveaxis(xs, 1, 2).reshape((b, seq) + rest)


def dilated_band_attention(q, kb, vb, dil, steps):
    s, dh = q.shape[1], q.shape[-1]
    qb = to_dilated_blocks(q, dil)
    n_blk = qb.shape[2]
    scores = jnp.einsum('brnqhd,brnkhd->brnhqk', qb, kb,
                        preferred_element_type=jnp.float32) * (dh ** -0.5)
    qi = jnp.arange(BAND_BLOCK)[:, None]
    kj = jnp.arange(2 * BAND_BLOCK)[None, :]
    dist = BAND_BLOCK + qi - kj
    blk = jnp.arange(n_blk)[:, None, None]
    valid = (dist >= 0) & (dist <= steps) & ((blk - 1) * BAND_BLOCK + kj >= 0)
    scores = jnp.where(valid[:, None], scores, -jnp.inf)
    m = jnp.max(scores, axis=-1, keepdims=True)
    e = jnp.exp(scores - m)
    den = jnp.sum(e, axis=-1, keepdims=True)
    o = jnp.einsum('brnhqk,brnkhd->brnqhd', (e / den).astype(vb.dtype), vb)
    lse = jnp.swapaxes((m + jnp.log(den))[..., 0], -1, -2)
    return from_dilated_blocks(o, s), from_dilated_blocks(lse, s)


def shared_kv_blocks(x, kv_norm_w, w_kv, positions):
    b, s, _ = x.shape
    kv = (rms_norm(x, kv_norm_w) @ w_kv).reshape(b, s, 2, N_GROUPS, B_HEADS_PER_GROUP, HEAD_DIM)
    k = partial_rotary(kv[:, :, 0], positions)
    v = kv[:, :, 1]
    k_blocks = [with_previous_block(to_dilated_blocks(k[:, :, g], dil))
                for g, (_, dil) in enumerate(DILATED_GROUPS)]
    v_blocks = [with_previous_block(to_dilated_blocks(v[:, :, g], dil))
                for g, (_, dil) in enumerate(DILATED_GROUPS)]
    return k_blocks, v_blocks


def mixer_b(h, w_q, w_o, k_blocks, v_blocks, positions):
    b, s, _ = h.shape
    q = partial_rotary((h @ w_q).reshape(b, s, N_GROUPS, B_HEADS_PER_GROUP, HEAD_DIM), positions)
    outs, lses = [], []
    for g, (window, dil) in enumerate(DILATED_GROUPS):
        o_g, lse_g = dilated_band_attention(q[:, :, g], k_blocks[g], v_blocks[g], dil, window // dil)
        outs.append(o_g)
        lses.append(lse_g)
    alpha = jax.nn.softmax(jnp.stack(lses, axis=2), axis=2)
    o = jnp.stack(outs, axis=2) * alpha[..., None].astype(outs[0].dtype)
    return o.reshape(b, s, B_WIDTH) @ w_o


def causal_depthwise_conv(u, w, bias):
    c = u.shape[-1]
    y = lax.conv_general_dilated(u, w[:, None, :].astype(u.dtype), window_strides=(1,),
                                 padding=[(CONV_WIDTH - 1, 0)],
                                 dimension_numbers=('NWC', 'WIO', 'NWC'),
                                 feature_group_count=c)
    return y + bias.astype(u.dtype)


def conv_ffn(h, w_up, conv_w, conv_b, w_down):
    u = causal_depthwise_conv(h @ w_up, conv_w, conv_b)
    gate, val = u[..., :D_FF], u[..., D_FF:]
    return (jax.nn.silu(gate) * val) @ w_down


def setup_inputs(seed: int = 0) -> dict:
    key = jax.random.key(seed)
    ks = jax.random.split(key, 20)

    def nrm(k, shape, scale=1.0):
        return jax.random.normal(k, shape, jnp.float32) * scale

    def gain(k, shape):
        return 1.0 + nrm(k, shape, 0.02)

    res = (2.0 * DEPTH) ** -0.5
    positions = (jnp.arange(SEQ, dtype=jnp.int32)[None, :]
                 + jax.random.randint(ks[2], (BATCH, 1), 0, 1024, dtype=jnp.int32))
    return {
        'x': nrm(ks[0], (BATCH, SEQ, D_MODEL)),
        'p': nrm(ks[1], (DEPTH, BATCH, SEQ, PLE_DIM)),
        'positions': positions,
        'attn_norm_w': gain(ks[3], (DEPTH, D_MODEL)),
        'a_w_qkv': nrm(ks[4], (N_A_LAYERS, D_MODEL, 3 * A_WIDTH), D_MODEL ** -0.5),
        'a_w_o': nrm(ks[5], (N_A_LAYERS, A_WIDTH, D_MODEL), res * A_WIDTH ** -0.5),
        'kv_norm_w': gain(ks[6], (D_MODEL,)),
        'b_w_kv': nrm(ks[7], (D_MODEL, 2 * B_WIDTH), D_MODEL ** -0.5),
        'b_w_q': nrm(ks[8], (N_B_LAYERS, D_MODEL, B_WIDTH), D_MODEL ** -0.5),
        'b_w_o': nrm(ks[9], (N_B_LAYERS, B_WIDTH, D_MODEL), res * B_WIDTH ** -0.5),
        'ffn_norm_w': gain(ks[10], (DEPTH, D_MODEL)),
        'ffn_w_up': nrm(ks[11], (DEPTH, D_MODEL, 2 * D_FF), D_MODEL ** -0.5),
        'ffn_conv_w': nrm(ks[12], (DEPTH, CONV_WIDTH, 2 * D_FF), CONV_WIDTH ** -0.5),
        'ffn_conv_b': nrm(ks[13], (DEPTH, 2 * D_FF), 0.01),
        'ffn_w_down': nrm(ks[14], (DEPTH, D_FF, D_MODEL), res * D_FF ** -0.5),
        'ple_norm_w': gain(ks[15], (DEPTH, D_MODEL)),
        'ple_w_gate': nrm(ks[16], (DEPTH, D_MODEL, D_MODEL), D_MODEL ** -0.5),
        'ple_w_proj': nrm(ks[17], (DEPTH, PLE_DIM, D_MODEL), res * PLE_DIM ** -0.5),
        'final_norm_w': gain(ks[18], (D_MODEL,)),
    }


def reference(x, p, positions, attn_norm_w, a_w_qkv, a_w_o, kv_norm_w, b_w_kv, b_w_q, b_w_o,
              ffn_norm_w, ffn_w_up, ffn_conv_w, ffn_conv_b, ffn_w_down,
              ple_norm_w, ple_w_gate, ple_w_proj, final_norm_w):
    k_blocks, v_blocks = None, None
    for i in range(DEPTH):
        h = rms_norm(x, attn_norm_w[i])
        if i < N_A_LAYERS:
            x = x + mixer_a(h, a_w_qkv[i], a_w_o[i])
        else:
            if i == N_A_LAYERS:
                k_blocks, v_blocks = shared_kv_blocks(x, kv_norm_w, b_w_kv, positions)
            j = i - N_A_LAYERS
            x = x + mixer_b(h, b_w_q[j], b_w_o[j], k_blocks, v_blocks, positions)
        x = x + conv_ffn(rms_norm(x, ffn_norm_w[i]), ffn_w_up[i], ffn_conv_w[i], ffn_conv_b[i], ffn_w_down[i])
        gate = jax.nn.sigmoid(rms_norm(x, ple_norm_w[i]) @ ple_w_gate[i])
        x = x + gate * (p[i].astype(x.dtype) @ ple_w_proj[i])
    return rms_norm(x, final_norm_w)
```

```python
import functools

import jax
import jax.numpy as jnp
from jax import lax
from jax.experimental import pallas as pl
from jax.experimental.pallas import tpu as pltpu

D_MODEL = 2048
DEPTH = 4
N_A_LAYERS = DEPTH // 2
HEAD_DIM = 128
A_HEADS = D_MODEL // HEAD_DIM
A_WIDTH = A_HEADS * HEAD_DIM
DILATED_GROUPS = ((128, 1), (512, 4), (2048, 16))
N_GROUPS = len(DILATED_GROUPS)
B_HEADS_PER_GROUP = 4
B_WIDTH = N_GROUPS * B_HEADS_PER_GROUP * HEAD_DIM
BAND_BLOCK = 128
ROT_DIM = HEAD_DIM // 4
ROPE_THETA = 500000.0
D_FF = 256 * ((8 * D_MODEL // 3 + 255) // 256)
CONV_WIDTH = 3
PLE_DIM = 256
NORM_EPS = 1e-6
ATTN_SCALE = HEAD_DIM ** -0.5

VMEM_LIMIT_BYTES = 56 * 1024 * 1024
NEG_INF = float("-inf")

_NT_DIMS = (((1,), (1,)), ((), ()))


def _compiler_params(semantics):
    return pltpu.CompilerParams(dimension_semantics=semantics, vmem_limit_bytes=VMEM_LIMIT_BYTES)


def _rms_norm_rows(x, g):
    y = x * lax.rsqrt(jnp.mean(x * x, axis=-1, keepdims=True) + NORM_EPS)
    return y * g


def _rotary_head(xh, cos, sin_lo, sin_hi):
    return (xh * cos
            + pltpu.roll(xh, HEAD_DIM - ROT_DIM // 2, axis=1) * sin_lo
            + pltpu.roll(xh, ROT_DIM // 2, axis=1) * sin_hi)


def _norm_proj_kernel(*refs, scale_tiles, rot_tiles, tn):
    if rot_tiles:
        x_ref, g_ref, w_ref, pos_ref, invf_ref, o_ref, h_ref, cos_ref, slo_ref, shi_ref = refs
    else:
        x_ref, g_ref, w_ref, o_ref, h_ref = refs
    j = pl.program_id(1)

    @pl.when(j == 0)
    def _():
        h_ref[...] = _rms_norm_rows(x_ref[...], g_ref[...]).astype(h_ref.dtype)
        if rot_tiles:
            ang = pos_ref[...].astype(jnp.float32) * invf_ref[...]
            lane = lax.broadcasted_iota(jnp.int32, ang.shape, 1)
            sin = jnp.sin(ang)
            cos_ref[...] = jnp.cos(ang)
            slo_ref[...] = jnp.where(lane < ROT_DIM // 2, -sin, 0.0)
            shi_ref[...] = jnp.where(lane >= ROT_DIM // 2, sin, 0.0)

    acc = jnp.dot(h_ref[...], w_ref[...], preferred_element_type=jnp.float32)
    if scale_tiles:
        acc = acc * jnp.where(j < scale_tiles, ATTN_SCALE, 1.0)
    if not rot_tiles:
        o_ref[...] = acc.astype(o_ref.dtype)
        return

    @pl.when(j < rot_tiles)
    def _():
        cos, slo, shi = cos_ref[...], slo_ref[...], shi_ref[...]
        for c in range(tn // HEAD_DIM):
            sl = slice(c * HEAD_DIM, (c + 1) * HEAD_DIM)
            o_ref[:, sl] = _rotary_head(acc[:, sl], cos, slo, shi).astype(o_ref.dtype)

    @pl.when(j >= rot_tiles)
    def _():
        o_ref[...] = acc.astype(o_ref.dtype)


def _norm_proj(x, g, w, *, out_dtype, scale_tiles=0, rot_tiles=0, pos=None, invf=None, tm=512, tn=512):
    t, d = x.shape
    n = w.shape[1]
    in_specs = [
        pl.BlockSpec((tm, d), lambda i, j: (i, 0)),
        pl.BlockSpec((1, d), lambda i, j: (0, 0)),
        pl.BlockSpec((d, tn), lambda i, j: (0, j)),
    ]
    args = [x, g.reshape(1, d), w]
    scratch = [pltpu.VMEM((tm, d), jnp.bfloat16)]
    if rot_tiles:
        in_specs += [pl.BlockSpec((tm, 1), lambda i, j: (i, 0)),
                     pl.BlockSpec((1, HEAD_DIM), lambda i, j: (0, 0))]
        args += [pos, invf]
        scratch += [pltpu.VMEM((tm, HEAD_DIM), jnp.float32)] * 3
    return pl.pallas_call(
        functools.partial(_norm_proj_kernel, scale_tiles=scale_tiles, rot_tiles=rot_tiles, tn=tn),
        grid=(t // tm, n // tn),
        in_specs=in_specs,
        out_specs=pl.BlockSpec((tm, tn), lambda i, j: (i, j)),
        out_shape=jax.ShapeDtypeStruct((t, n), out_dtype),
        scratch_shapes=scratch,
        compiler_params=_compiler_params(("arbitrary", "arbitrary")),
    )(*args)


def _stick_kernel(q_ref, k_ref, v_ref, u_ref, o_ref, *, tq):
    qi = pl.program_id(2)
    q = q_ref[...]
    u = u_ref[...]
    row = lax.broadcasted_iota(jnp.int32, (tq, tq), 0)
    col = lax.broadcasted_iota(jnp.int32, (tq, tq), 1)
    causal = col < row

    def key_block(j, carry, acc, masked):
        start = pl.multiple_of(j * tq, tq)
        k = k_ref[pl.ds(start, tq), :]
        v = v_ref[pl.ds(start, tq), :]
        z = lax.dot_general(q, k, _NT_DIMS, preferred_element_type=jnp.float32)
        l1m = jnp.minimum(-z, 0.0) - jnp.log(1.0 + jnp.exp(-jnp.abs(z)))
        if masked:
            l1m = jnp.where(causal, l1m, 0.0)
        hi = l1m.astype(jnp.bfloat16)
        lo = (l1m - hi.astype(jnp.float32)).astype(jnp.bfloat16)
        csum = (jnp.dot(hi, u, preferred_element_type=jnp.float32)
                + jnp.dot(lo, u, preferred_element_type=jnp.float32))
        a = jnp.exp(z + csum + carry)
        if masked:
            a = jnp.where(causal, a, 0.0)
        acc = acc + jnp.dot(a.astype(jnp.bfloat16), v, preferred_element_type=jnp.float32)
        return carry + csum[:, :1], acc

    carry0 = jnp.zeros((tq, 1), jnp.float32)
    acc0 = jnp.zeros((tq, HEAD_DIM), jnp.float32)
    carry, acc = key_block(qi, carry0, acc0, True)

    def body(jj, state):
        return key_block(qi - 1 - jj, state[0], state[1], False)

    carry, acc = lax.fori_loop(0, qi, body, (carry, acc))
    o_ref[...] = acc.astype(o_ref.dtype)


def _stick_attention(qkv, batch, seq, *, tq=256):
    nq = seq // tq
    tri = jnp.tril(jnp.ones((tq, tq), jnp.bfloat16))
    return pl.pallas_call(
        functools.partial(_stick_kernel, tq=tq),
        grid=(batch, A_HEADS, nq),
        in_specs=[
            pl.BlockSpec((tq, HEAD_DIM), lambda b, h, i: (b * nq + i, h)),
            pl.BlockSpec((seq, HEAD_DIM), lambda b, h, i: (b, A_HEADS + h)),
            pl.BlockSpec((seq, HEAD_DIM), lambda b, h, i: (b, 2 * A_HEADS + h)),
            pl.BlockSpec((tq, tq), lambda b, h, i: (0, 0)),
        ],
        out_specs=pl.BlockSpec((tq, HEAD_DIM), lambda b, h, i: (b * nq + i, h)),
        out_shape=jax.ShapeDtypeStruct((batch * seq, A_WIDTH), jnp.bfloat16),
        compiler_params=_compiler_params(("arbitrary", "arbitrary", "arbitrary")),
    )(qkv, qkv, qkv, tri)


def _band_kernel(q0, q1, q2, k0, k1, k2, v0, v1, v2, o0, o1, o2, o_scr, l_scr, *, seq):
    q_refs, k_refs, v_refs, o_refs = (q0, q1, q2), (k0, k1, k2), (v0, v1, v2), (o0, o1, o2)
    blk = BAND_BLOCK
    for g, (window, dil) in enumerate(DILATED_GROUPS):
        steps = window // dil
        n_blk = seq // dil // blk
        kw = 2 * blk if n_blk > 1 else blk
        q_ref, k_ref, v_ref = q_refs[g], k_refs[g], v_refs[g]
        qpos = lax.broadcasted_iota(jnp.int32, (blk, kw), 0)
        kpos = lax.broadcasted_iota(jnp.int32, (blk, kw), 1)

        def body(it, _, g=g, dil=dil, steps=steps, n_blk=n_blk, kw=kw, q_ref=q_ref, k_ref=k_ref,
                 v_ref=v_ref, qpos=qpos, kpos=kpos):
            r = it // n_blk
            n = it % n_blk
            k_first = jnp.maximum(n - 1, 0) * blk
            q_rows = pl.ds(r + dil * blk * n, blk, stride=dil)
            k_rows = pl.ds(r + dil * k_first, kw, stride=dil)
            qb = q_ref[q_rows, :].astype(jnp.bfloat16)
            kb = k_ref[k_rows, :].astype(jnp.bfloat16)
            vb = v_ref[k_rows, :].astype(jnp.bfloat16)
            s = lax.dot_general(qb, kb, _NT_DIMS, preferred_element_type=jnp.float32)
            dist = (n * blk - k_first) + qpos - kpos
            s = jnp.where((dist >= 0) & (dist <= steps), s, NEG_INF)
            m = jnp.max(s, axis=-1, keepdims=True)
            e = jnp.exp(s - m)
            den = jnp.sum(e, axis=-1, keepdims=True)
            o = jnp.dot(e.astype(jnp.bfloat16), vb, preferred_element_type=jnp.float32) / den
            o_scr[g, q_rows, :] = o
            l_scr[g, q_rows, :] = jnp.broadcast_to(m + jnp.log(den), (blk, HEAD_DIM))
            return 0

        lax.fori_loop(0, dil * n_blk, body, 0)

    chunk = 256
    def mix(c, _):
        rows = pl.ds(pl.multiple_of(c * chunk, chunk), chunk)
        ls = [l_scr[g, rows, :] for g in range(N_GROUPS)]
        mx = jnp.maximum(jnp.maximum(ls[0], ls[1]), ls[2])
        es = [jnp.exp(l - mx) for l in ls]
        tot = es[0] + es[1] + es[2]
        for g in range(N_GROUPS):
            o_refs[g][rows, :] = (o_scr[g, rows, :] * (es[g] / tot)).astype(o_refs[g].dtype)
        return 0

    lax.fori_loop(0, seq // chunk, mix, 0)


def _band_attention(q, kv, batch, seq):
    hg = B_HEADS_PER_GROUP
    n_heads = N_GROUPS * hg

    def col(base, g):
        return lambda b, h: (b, base + g * hg + h)

    blockspec = lambda base, g: pl.BlockSpec((seq, HEAD_DIM), col(base, g))
    return pl.pallas_call(
        functools.partial(_band_kernel, seq=seq),
        grid=(batch, hg),
        in_specs=([blockspec(0, g) for g in range(N_GROUPS)]
                  + [blockspec(0, g) for g in range(N_GROUPS)]
                  + [blockspec(n_heads, g) for g in range(N_GROUPS)]),
        out_specs=[pl.BlockSpec((seq, HEAD_DIM), lambda b, h: (b, h))] * N_GROUPS,
        out_shape=[jax.ShapeDtypeStruct((batch * seq, hg * HEAD_DIM), jnp.bfloat16)] * N_GROUPS,
        scratch_shapes=[pltpu.VMEM((N_GROUPS, seq, HEAD_DIM), jnp.float32)] * 2,
        compiler_params=_compiler_params(("arbitrary", "arbitrary")),
    )(q, q, q, kv, kv, kv, kv, kv, kv)


def _proj_residual_kernel(*refs, n_parts):
    a_refs, w_ref, x_ref, o_ref = refs[:n_parts], refs[n_parts], refs[n_parts + 1], refs[n_parts + 2]
    acc = x_ref[...]
    gw = w_ref.shape[0] // n_parts
    for g in range(n_parts):
        acc = acc + jnp.dot(a_refs[g][...], w_ref[g * gw:(g + 1) * gw, :], preferred_element_type=jnp.float32)
    o_ref[...] = acc


def _proj_residual(a_parts, w, x, *, tm=512):
    t, d = x.shape
    kdim = w.shape[0]
    n_parts = len(a_parts)
    return pl.pallas_call(
        functools.partial(_proj_residual_kernel, n_parts=n_parts),
        grid=(t // tm,),
        in_specs=([pl.BlockSpec((tm, kdim // n_parts), lambda i: (i, 0))] * n_parts
                  + [pl.BlockSpec((kdim, d), lambda i: (0, 0)),
                     pl.BlockSpec((tm, d), lambda i: (i, 0))]),
        out_specs=pl.BlockSpec((tm, d), lambda i: (i, 0)),
        out_shape=jax.ShapeDtypeStruct((t, d), jnp.float32),
        compiler_params=_compiler_params(("arbitrary",)),
    )(*a_parts, w, x)


def _conv_ffn_kernel(x_ref, g_ref, wg_ref, wv_ref, cwg_ref, cwv_ref, cbg_ref, cbv_ref, wd_ref,
                     o_ref, h_ref, tail_ref, *, tm, tiles_per_seq):
    i = pl.program_id(0)
    j = pl.program_id(1)

    @pl.when(j == 0)
    def _():
        x = x_ref[...]
        h_ref[...] = _rms_norm_rows(x, g_ref[...]).astype(h_ref.dtype)
        o_ref[...] = x

    @pl.when((i % tiles_per_seq) == 0)
    def _():
        tail_ref[:, j] = jnp.zeros((2, 8, tail_ref.shape[-1]), jnp.float32)

    rid = lax.broadcasted_iota(jnp.int32, (tm, 1), 0)

    def conv_half(w_ref, cw_ref, cb_ref, slot):
        u = jnp.dot(h_ref[...], w_ref[...], preferred_element_type=jnp.float32)
        prev = tail_ref[slot, j]
        p1 = prev[7:8, :]
        p2 = prev[6:7, :]
        um1 = jnp.where(rid == 0, p1, pltpu.roll(u, 1, axis=0))
        um2 = jnp.where(rid == 0, p2, jnp.where(rid == 1, p1, pltpu.roll(u, 2, axis=0)))
        tail_ref[slot, j] = u[tm - 8:, :]
        cw = cw_ref[...]
        return cw[0:1, :] * um2 + cw[1:2, :] * um1 + cw[2:3, :] * u + cb_ref[...]

    gate = conv_half(wg_ref, cwg_ref, cbg_ref, 0)
    val = conv_half(wv_ref, cwv_ref, cbv_ref, 1)
    act = (gate * (1.0 / (1.0 + jnp.exp(-gate))) * val).astype(jnp.bfloat16)
    o_ref[...] += jnp.dot(act, wd_ref[...], preferred_element_type=jnp.float32)


def _conv_ffn(x, g, w_up, conv_w, conv_b, w_down, seq, *, tm=512, tf=512):
    t, d = x.shape
    nj = D_FF // tf
    return pl.pallas_call(
        functools.partial(_conv_ffn_kernel, tm=tm, tiles_per_seq=seq // tm),
        grid=(t // tm, nj),
        in_specs=[
            pl.BlockSpec((tm, d), lambda i, j: (i, 0)),
            pl.BlockSpec((1, d), lambda i, j: (0, 0)),
            pl.BlockSpec((d, tf), lambda i, j: (0, j)),
            pl.BlockSpec((d, tf), lambda i, j: (0, nj + j)),
            pl.BlockSpec((CONV_WIDTH, tf), lambda i, j: (0, j)),
            pl.BlockSpec((CONV_WIDTH, tf), lambda i, j: (0, nj + j)),
            pl.BlockSpec((1, tf), lambda i, j: (0, j)),
            pl.BlockSpec((1, tf), lambda i, j: (0, nj + j)),
            pl.BlockSpec((tf, d), lambda i, j: (j, 0)),
        ],
        out_specs=pl.BlockSpec((tm, d), lambda i, j: (i, 0)),
        out_shape=jax.ShapeDtypeStruct((t, d), jnp.float32),
        scratch_shapes=[pltpu.VMEM((tm, d), jnp.bfloat16),
                        pltpu.VMEM((2, nj, 8, tf), jnp.float32)],
        compiler_params=_compiler_params(("arbitrary", "arbitrary")),
    )(x, g.reshape(1, d), w_up, w_up, conv_w, conv_w, conv_b.reshape(1, -1), conv_b.reshape(1, -1), w_down)


def _ple_kernel(*refs, final_norm):
    if final_norm:
        x_ref, g_ref, wg_ref, p_ref, wp_ref, fg_ref, o_ref = refs
    else:
        x_ref, g_ref, wg_ref, p_ref, wp_ref, o_ref = refs
    x = x_ref[...]
    h = _rms_norm_rows(x, g_ref[...]).astype(jnp.bfloat16)
    gate = jnp.dot(h, wg_ref[...], preferred_element_type=jnp.float32)
    gate = 1.0 / (1.0 + jnp.exp(-gate))
    proj = jnp.dot(p_ref[...].astype(jnp.bfloat16), wp_ref[...], preferred_element_type=jnp.float32)
    y = x + gate * proj
    if final_norm:
        y = _rms_norm_rows(y, fg_ref[...])
    o_ref[...] = y


def _ple(x, g, w_gate, p, w_proj, final_g=None, *, tm=256):
    t, d = x.shape
    final_norm = final_g is not None
    in_specs = [
        pl.BlockSpec((tm, d), lambda i: (i, 0)),
        pl.BlockSpec((1, d), lambda i: (0, 0)),
        pl.BlockSpec((d, d), lambda i: (0, 0)),
        pl.BlockSpec((tm, PLE_DIM), lambda i: (i, 0)),
        pl.BlockSpec((PLE_DIM, d), lambda i: (0, 0)),
    ]
    args = [x, g.reshape(1, d), w_gate, p, w_proj]
    if final_norm:
        in_specs.append(pl.BlockSpec((1, d), lambda i: (0, 0)))
        args.append(final_g.reshape(1, d))
    return pl.pallas_call(
        functools.partial(_ple_kernel, final_norm=final_norm),
        grid=(t // tm,),
        in_specs=in_specs,
        out_specs=pl.BlockSpec((tm, d), lambda i: (i, 0)),
        out_shape=jax.ShapeDtypeStruct((t, d), jnp.float32),
        compiler_params=_compiler_params(("arbitrary",)),
    )(*args)


def kernel(x, p, positions, attn_norm_w, a_w_qkv, a_w_o, kv_norm_w, b_w_kv, b_w_q, b_w_o, ffn_norm_w,
           ffn_w_up, ffn_conv_w, ffn_conv_b, ffn_w_down, ple_norm_w, ple_w_gate, ple_w_proj, final_norm_w):
    batch, seq, d = x.shape
    t = batch * seq
    bf = lambda w: w.astype(jnp.bfloat16)
    xf = x.reshape(t, d)
    pos = positions.reshape(t, 1)
    inv_freq = ROPE_THETA ** (-jnp.arange(0, ROT_DIM, 2, dtype=jnp.float32) / ROT_DIM)
    invf = jnp.zeros((1, HEAD_DIM), jnp.float32).at[0, :ROT_DIM].set(jnp.tile(inv_freq, 2))

    kv = None
    for i in range(DEPTH):
        if i < N_A_LAYERS:
            qkv = _norm_proj(xf, attn_norm_w[i], bf(a_w_qkv[i]), out_dtype=jnp.bfloat16,
                             scale_tiles=A_WIDTH // 512)
            o = _stick_attention(qkv, batch, seq)
            xf = _proj_residual([o], bf(a_w_o[i]), xf)
        else:
            if i == N_A_LAYERS:
                kv = _norm_proj(xf, kv_norm_w, bf(b_w_kv), out_dtype=jnp.float32,
                                rot_tiles=B_WIDTH // 512, pos=pos, invf=invf)
            jb = i - N_A_LAYERS
            q = _norm_proj(xf, attn_norm_w[i], bf(b_w_q[jb]), out_dtype=jnp.float32,
                           scale_tiles=B_WIDTH // 512, rot_tiles=B_WIDTH // 512, pos=pos, invf=invf)
            o_parts = _band_attention(q, kv, batch, seq)
            xf = _proj_residual(o_parts, bf(b_w_o[jb]), xf)
        xf = _conv_ffn(xf, ffn_norm_w[i], bf(ffn_w_up[i]), ffn_conv_w[i], ffn_conv_b[i], bf(ffn_w_down[i]), seq)
        xf = _ple(xf, ple_norm_w[i], bf(ple_w_gate[i]), p[i].reshape(t, PLE_DIM), bf(ple_w_proj[i]),
                  final_norm_w if i == DEPTH - 1 else None)
    return xf.reshape(batch, seq, d)
```

```python
import functools

import jax
import jax.numpy as jnp
from jax import lax
from jax.experimental import pallas as pl
from jax.experimental.pallas import tpu as pltpu

D_MODEL = 2048
DEPTH = 4
N_A_LAYERS = DEPTH // 2
HEAD_DIM = 128
A_HEADS = D_MODEL // HEAD_DIM
A_WIDTH = A_HEADS * HEAD_DIM
DILATED_GROUPS = ((128, 1), (512, 4), (2048, 16))
N_GROUPS = len(DILATED_GROUPS)
B_HEADS_PER_GROUP = 4
B_WIDTH = N_GROUPS * B_HEADS_PER_GROUP * HEAD_DIM
BAND_BLOCK = 128
ROT_DIM = HEAD_DIM // 4
ROPE_THETA = 500000.0
D_FF = 256 * ((8 * D_MODEL // 3 + 255) // 256)
CONV_WIDTH = 3
PLE_DIM = 256
NORM_EPS = 1e-6
ATTN_SCALE = HEAD_DIM ** -0.5
LOG2_E = 1.4426950408889634
STICK_DONE_LOG2 = 151.0

VMEM_LIMIT_BYTES = 56 * 1024 * 1024
NEG_INF = float("-inf")

_NT_DIMS = (((1,), (1,)), ((), ()))


def _compiler_params(semantics):
    return pltpu.CompilerParams(dimension_semantics=semantics, vmem_limit_bytes=VMEM_LIMIT_BYTES)


def _rms_norm_rows(x, g):
    y = x * lax.rsqrt(jnp.mean(x * x, axis=-1, keepdims=True) + NORM_EPS)
    return y * g


def _rotary_head(xh, cos, sin_lo, sin_hi):
    return (xh * cos
            + pltpu.roll(xh, HEAD_DIM - ROT_DIM // 2, axis=1) * sin_lo
            + pltpu.roll(xh, ROT_DIM // 2, axis=1) * sin_hi)


def _norm_proj_kernel(*refs, scale, scale_tiles, rot_tiles, tn):
    if rot_tiles:
        x_ref, g_ref, w_ref, pos_ref, invf_ref, o_ref, h_ref, cos_ref, slo_ref, shi_ref = refs
    else:
        x_ref, g_ref, w_ref, o_ref, h_ref = refs
    j = pl.program_id(1)

    @pl.when(j == 0)
    def _():
        h_ref[...] = _rms_norm_rows(x_ref[...], g_ref[...]).astype(h_ref.dtype)
        if rot_tiles:
            ang = pos_ref[...].astype(jnp.float32) * invf_ref[...]
            lane = lax.broadcasted_iota(jnp.int32, ang.shape, 1)
            sin = jnp.sin(ang)
            cos_ref[...] = jnp.cos(ang)
            slo_ref[...] = jnp.where(lane < ROT_DIM // 2, -sin, 0.0)
            shi_ref[...] = jnp.where(lane >= ROT_DIM // 2, sin, 0.0)

    acc = jnp.dot(h_ref[...], w_ref[...], preferred_element_type=jnp.float32)
    if scale_tiles:
        acc = acc * jnp.where(j < scale_tiles, scale, 1.0)
    if not rot_tiles:
        o_ref[...] = acc.astype(o_ref.dtype)
        return

    @pl.when(j < rot_tiles)
    def _():
        cos, slo, shi = cos_ref[...], slo_ref[...], shi_ref[...]
        for c in range(tn // HEAD_DIM):
            sl = slice(c * HEAD_DIM, (c + 1) * HEAD_DIM)
            o_ref[:, sl] = _rotary_head(acc[:, sl], cos, slo, shi).astype(o_ref.dtype)

    @pl.when(j >= rot_tiles)
    def _():
        o_ref[...] = acc.astype(o_ref.dtype)


def _norm_proj(x, g, w, *, name, out_dtype, scale=1.0, scale_tiles=0, rot_tiles=0, pos=None, invf=None,
               tm=512, tn=512):
    t, d = x.shape
    n = w.shape[1]
    in_specs = [
        pl.BlockSpec((tm, d), lambda i, j: (i, 0)),
        pl.BlockSpec((1, d), lambda i, j: (0, 0)),
        pl.BlockSpec((d, tn), lambda i, j: (0, j)),
    ]
    args = [x, g.reshape(1, d), w]
    scratch = [pltpu.VMEM((tm, d), jnp.bfloat16)]
    if rot_tiles:
        in_specs += [pl.BlockSpec((tm, 1), lambda i, j: (i, 0)),
                     pl.BlockSpec((1, HEAD_DIM), lambda i, j: (0, 0))]
        args += [pos, invf]
        scratch += [pltpu.VMEM((tm, HEAD_DIM), jnp.float32)] * 3
    return pl.pallas_call(
        functools.partial(_norm_proj_kernel, scale=scale, scale_tiles=scale_tiles, rot_tiles=rot_tiles, tn=tn),
        grid=(t // tm, n // tn),
        in_specs=in_specs,
        out_specs=pl.BlockSpec((tm, tn), lambda i, j: (i, j)),
        out_shape=jax.ShapeDtypeStruct((t, n), out_dtype),
        scratch_shapes=scratch,
        compiler_params=_compiler_params(("arbitrary", "arbitrary")),
        name=name,
    )(*args)


def _stick_kernel(q_ref, k_ref, v_ref, u2_ref, o_ref, acc_ref, carry_ref, *, tq):
    qi = pl.program_id(2)
    q = q_ref[...]
    row = lax.broadcasted_iota(jnp.int32, (tq, tq), 0)
    col = lax.broadcasted_iota(jnp.int32, (tq, tq), 1)
    causal = col < row

    def window(start, n_chunks, diag, carry):
        rows = pl.ds(pl.multiple_of(start, tq), n_chunks * tq)
        z = lax.dot_general(q, k_ref[rows, :], _NT_DIMS, preferred_element_type=jnp.float32)
        parts = [None] * n_chunks
        for c in reversed(range(n_chunks)):
            zc = z[:, c * tq:(c + 1) * tq]
            sp = jnp.maximum(zc, 0.0) + jnp.log2(1.0 + jnp.exp2(-jnp.abs(zc)))
            masked = diag and c == n_chunks - 1
            if masked:
                sp = jnp.where(causal, sp, 0.0)
            hi = sp.astype(jnp.bfloat16)
            lo = (sp - hi.astype(jnp.float32)).astype(jnp.bfloat16)
            csum = jnp.dot(jnp.concatenate([hi, lo], axis=1), u2_ref[...],
                           preferred_element_type=jnp.float32)
            arg = zc - csum if carry is None else zc - csum - carry
            a = jnp.exp2(arg)
            if masked:
                a = jnp.where(causal, a, 0.0)
            parts[c] = a.astype(jnp.bfloat16)
            carry = csum[:, :1] if carry is None else carry + csum[:, :1]
        a = parts[0] if n_chunks == 1 else jnp.concatenate(parts, axis=1)
        return jnp.dot(a, v_ref[rows, :], preferred_element_type=jnp.float32), carry

    @pl.when(qi == 0)
    def _():
        acc, _ = window(0, 1, True, None)
        o_ref[...] = acc.astype(o_ref.dtype)

    @pl.when(qi > 0)
    def _():
        acc, carry = window((qi - 1) * tq, 2, True, None)
        acc_ref[...] = acc
        carry_ref[...] = carry

        def more(state):
            j, reach = state
            return (j >= 0) & (reach < STICK_DONE_LOG2)

        def step(state):
            j, _ = state
            acc, carry = window(j * tq, 1, False, carry_ref[...])
            acc_ref[...] += acc
            carry_ref[...] = carry
            return j - 1, jnp.min(carry)

        lax.while_loop(more, step, (qi - 2, jnp.min(carry)))
        o_ref[...] = acc_ref[...].astype(o_ref.dtype)


def _stick_attention(qkv, batch, seq, *, tq=256):
    nq = seq // tq
    tri = jnp.tril(jnp.ones((tq, tq), jnp.bfloat16))
    return pl.pallas_call(
        functools.partial(_stick_kernel, tq=tq),
        grid=(batch, A_HEADS, nq),
        in_specs=[
            pl.BlockSpec((tq, HEAD_DIM), lambda b, h, i: (b * nq + i, h)),
            pl.BlockSpec((seq, HEAD_DIM), lambda b, h, i: (b, A_HEADS + h)),
            pl.BlockSpec((seq, HEAD_DIM), lambda b, h, i: (b, 2 * A_HEADS + h)),
            pl.BlockSpec((2 * tq, tq), lambda b, h, i: (0, 0)),
        ],
        out_specs=pl.BlockSpec((tq, HEAD_DIM), lambda b, h, i: (b * nq + i, h)),
        out_shape=jax.ShapeDtypeStruct((batch * seq, A_WIDTH), jnp.bfloat16),
        scratch_shapes=[pltpu.VMEM((tq, HEAD_DIM), jnp.float32), pltpu.VMEM((tq, 1), jnp.float32)],
        compiler_params=_compiler_params(("arbitrary", "arbitrary", "arbitrary")),
        name="stick_attention",
    )(qkv, qkv, qkv, jnp.concatenate([tri, tri], axis=0))


def _band_kernel(q0, q1, q2, k0, k1, k2, v0, v1, v2, o0, o1, o2, o_scr, l_scr, *, seq):
    q_refs, k_refs, v_refs, o_refs = (q0, q1, q2), (k0, k1, k2), (v0, v1, v2), (o0, o1, o2)
    blk = BAND_BLOCK
    for g, (window, dil) in enumerate(DILATED_GROUPS):
        steps = window // dil
        n_blk = seq // dil // blk
        kw = 2 * blk if n_blk > 1 else blk
        q_ref, k_ref, v_ref = q_refs[g], k_refs[g], v_refs[g]
        qpos = lax.broadcasted_iota(jnp.int32, (blk, kw), 0)
        kpos = lax.broadcasted_iota(jnp.int32, (blk, kw), 1)

        def body(it, _, g=g, dil=dil, steps=steps, n_blk=n_blk, kw=kw, q_ref=q_ref, k_ref=k_ref,
                 v_ref=v_ref, qpos=qpos, kpos=kpos):
            r = it // n_blk
            n = it % n_blk
            k_first = jnp.maximum(n - 1, 0) * blk
            q_rows = pl.ds(r + dil * blk * n, blk, stride=dil)
            k_rows = pl.ds(r + dil * k_first, kw, stride=dil)
            qb = q_ref[q_rows, :].astype(jnp.bfloat16)
            kb = k_ref[k_rows, :].astype(jnp.bfloat16)
            vb = v_ref[k_rows, :].astype(jnp.bfloat16)
            s = lax.dot_general(qb, kb, _NT_DIMS, preferred_element_type=jnp.float32)
            dist = (n * blk - k_first) + qpos - kpos
            s = jnp.where((dist >= 0) & (dist <= steps), s, NEG_INF)
            m = jnp.max(s, axis=-1, keepdims=True)
            e = jnp.exp(s - m)
            den = jnp.sum(e, axis=-1, keepdims=True)
            o = jnp.dot(e.astype(jnp.bfloat16), vb, preferred_element_type=jnp.float32) / den
            o_scr[g, q_rows, :] = o
            l_scr[g, q_rows, :] = jnp.broadcast_to(m + jnp.log(den), (blk, HEAD_DIM))
            return 0

        lax.fori_loop(0, dil * n_blk, body, 0, unroll=4)

    chunk = 256
    def mix(c, _):
        rows = pl.ds(pl.multiple_of(c * chunk, chunk), chunk)
        ls = [l_scr[g, rows, :] for g in range(N_GROUPS)]
        mx = jnp.maximum(jnp.maximum(ls[0], ls[1]), ls[2])
        es = [jnp.exp(l - mx) for l in ls]
        tot = es[0] + es[1] + es[2]
        for g in range(N_GROUPS):
            o_refs[g][rows, :] = (o_scr[g, rows, :] * (es[g] / tot)).astype(o_refs[g].dtype)
        return 0

    lax.fori_loop(0, seq // chunk, mix, 0)


def _band_attention(q, kv, batch, seq):
    hg = B_HEADS_PER_GROUP
    n_heads = N_GROUPS * hg

    def col(base, g):
        return lambda b, h: (b, base + g * hg + h)

    blockspec = lambda base, g: pl.BlockSpec((seq, HEAD_DIM), col(base, g))
    return pl.pallas_call(
        functools.partial(_band_kernel, seq=seq),
        grid=(batch, hg),
        in_specs=([blockspec(0, g) for g in range(N_GROUPS)]
                  + [blockspec(0, g) for g in range(N_GROUPS)]
                  + [blockspec(n_heads, g) for g in range(N_GROUPS)]),
        out_specs=[pl.BlockSpec((seq, HEAD_DIM), lambda b, h: (b, h))] * N_GROUPS,
        out_shape=[jax.ShapeDtypeStruct((batch * seq, hg * HEAD_DIM), jnp.bfloat16)] * N_GROUPS,
        scratch_shapes=[pltpu.VMEM((N_GROUPS, seq, HEAD_DIM), jnp.float32)] * 2,
        compiler_params=_compiler_params(("arbitrary", "arbitrary")),
        name="band_attention",
    )(q, q, q, kv, kv, kv, kv, kv, kv)


def _proj_residual_kernel(*refs, n_parts):
    a_refs, w_ref, x_ref, o_ref = refs[:n_parts], refs[n_parts], refs[n_parts + 1], refs[n_parts + 2]
    acc = x_ref[...]
    gw = w_ref.shape[0] // n_parts
    for g in range(n_parts):
        acc = acc + jnp.dot(a_refs[g][...], w_ref[g * gw:(g + 1) * gw, :], preferred_element_type=jnp.float32)
    o_ref[...] = acc


def _proj_residual(a_parts, w, x, *, tm=512):
    t, d = x.shape
    kdim = w.shape[0]
    n_parts = len(a_parts)
    return pl.pallas_call(
        functools.partial(_proj_residual_kernel, n_parts=n_parts),
        grid=(t // tm,),
        in_specs=([pl.BlockSpec((tm, kdim // n_parts), lambda i: (i, 0))] * n_parts
                  + [pl.BlockSpec((kdim, d), lambda i: (0, 0)),
                     pl.BlockSpec((tm, d), lambda i: (i, 0))]),
        out_specs=pl.BlockSpec((tm, d), lambda i: (i, 0)),
        out_shape=jax.ShapeDtypeStruct((t, d), jnp.float32),
        compiler_params=_compiler_params(("arbitrary",)),
        name="proj_residual",
    )(*a_parts, w, x)


def _conv_ffn_kernel(x_ref, g_ref, wg_ref, wv_ref, cwg_ref, cwv_ref, cbg_ref, cbv_ref, wd_ref,
                     o_ref, h_ref, tail_ref, *, tm, tiles_per_seq, n_sub):
    i = pl.program_id(0)
    j = pl.program_id(1)

    @pl.when(j == 0)
    def _():
        x = x_ref[...]
        h_ref[...] = _rms_norm_rows(x, g_ref[...]).astype(h_ref.dtype)
        o_ref[...] = x

    @pl.when((i % tiles_per_seq) == 0)
    def _():
        tail_ref[:, j] = jnp.zeros((2, 8, tail_ref.shape[-1]), jnp.float32)

    rid = lax.broadcasted_iota(jnp.int32, (tm, 1), 0)

    def conv_half(w_ref, cw_ref, cb_ref, slot, cols):
        u = jnp.dot(h_ref[...], w_ref[:, cols], preferred_element_type=jnp.float32)
        prev = tail_ref[slot, j, :, cols]
        p1 = prev[7:8, :]
        p2 = prev[6:7, :]
        um1 = jnp.where(rid == 0, p1, pltpu.roll(u, 1, axis=0))
        um2 = jnp.where(rid == 0, p2, jnp.where(rid == 1, p1, pltpu.roll(u, 2, axis=0)))
        tail_ref[slot, j, :, cols] = u[tm - 8:, :]
        cw = cw_ref[:, cols]
        return cw[0:1, :] * um2 + cw[1:2, :] * um1 + cw[2:3, :] * u + cb_ref[:, cols]

    sub = wg_ref.shape[1] // n_sub
    acc = None
    for s in range(n_sub):
        cols = slice(s * sub, (s + 1) * sub)
        gate = conv_half(wg_ref, cwg_ref, cbg_ref, 0, cols)
        val = conv_half(wv_ref, cwv_ref, cbv_ref, 1, cols)
        act = (gate * (1.0 / (1.0 + jnp.exp(-gate))) * val).astype(jnp.bfloat16)
        part = jnp.dot(act, wd_ref[cols, :], preferred_element_type=jnp.float32)
        acc = part if acc is None else acc + part
    o_ref[...] += acc


def _conv_ffn(x, g, w_up, conv_w, conv_b, w_down, seq, *, tm=512, tf=512, n_sub=1):
    t, d = x.shape
    nj = D_FF // tf
    return pl.pallas_call(
        functools.partial(_conv_ffn_kernel, tm=tm, tiles_per_seq=seq // tm, n_sub=n_sub),
        grid=(t // tm, nj),
        in_specs=[
            pl.BlockSpec((tm, d), lambda i, j: (i, 0)),
            pl.BlockSpec((1, d), lambda i, j: (0, 0)),
            pl.BlockSpec((d, tf), lambda i, j: (0, j)),
            pl.BlockSpec((d, tf), lambda i, j: (0, nj + j)),
            pl.BlockSpec((CONV_WIDTH, tf), lambda i, j: (0, j)),
            pl.BlockSpec((CONV_WIDTH, tf), lambda i, j: (0, nj + j)),
            pl.BlockSpec((1, tf), lambda i, j: (0, j)),
            pl.BlockSpec((1, tf), lambda i, j: (0, nj + j)),
            pl.BlockSpec((tf, d), lambda i, j: (j, 0)),
        ],
        out_specs=pl.BlockSpec((tm, d), lambda i, j: (i, 0)),
        out_shape=jax.ShapeDtypeStruct((t, d), jnp.float32),
        scratch_shapes=[pltpu.VMEM((tm, d), jnp.bfloat16),
                        pltpu.VMEM((2, nj, 8, tf), jnp.float32)],
        compiler_params=_compiler_params(("arbitrary", "arbitrary")),
        name="conv_ffn",
    )(x, g.reshape(1, d), w_up, w_up, conv_w, conv_w, conv_b.reshape(1, -1), conv_b.reshape(1, -1), w_down)


def _ple_kernel(*refs, final_norm):
    if final_norm:
        x_ref, g_ref, wg_ref, p_ref, wp_ref, fg_ref, o_ref = refs
    else:
        x_ref, g_ref, wg_ref, p_ref, wp_ref, o_ref = refs
    x = x_ref[...]
    h = _rms_norm_rows(x, g_ref[...]).astype(jnp.bfloat16)
    gate = jnp.dot(h, wg_ref[...], preferred_element_type=jnp.float32)
    gate = 1.0 / (1.0 + jnp.exp(-gate))
    proj = jnp.dot(p_ref[...].astype(jnp.bfloat16), wp_ref[...], preferred_element_type=jnp.float32)
    y = x + gate * proj
    if final_norm:
        y = _rms_norm_rows(y, fg_ref[...])
    o_ref[...] = y


def _ple(x, g, w_gate, p, w_proj, final_g=None, *, tm=256):
    t, d = x.shape
    final_norm = final_g is not None
    in_specs = [
        pl.BlockSpec((tm, d), lambda i: (i, 0)),
        pl.BlockSpec((1, d), lambda i: (0, 0)),
        pl.BlockSpec((d, d), lambda i: (0, 0)),
        pl.BlockSpec((tm, PLE_DIM), lambda i: (i, 0)),
        pl.BlockSpec((PLE_DIM, d), lambda i: (0, 0)),
    ]
    args = [x, g.reshape(1, d), w_gate, p, w_proj]
    if final_norm:
        in_specs.append(pl.BlockSpec((1, d), lambda i: (0, 0)))
        args.append(final_g.reshape(1, d))
    return pl.pallas_call(
        functools.partial(_ple_kernel, final_norm=final_norm),
        grid=(t // tm,),
        in_specs=in_specs,
        out_specs=pl.BlockSpec((tm, d), lambda i: (i, 0)),
        out_shape=jax.ShapeDtypeStruct((t, d), jnp.float32),
        compiler_params=_compiler_params(("arbitrary",)),
        name="ple",
    )(*args)


def kernel(x, p, positions, attn_norm_w, a_w_qkv, a_w_o, kv_norm_w, b_w_kv, b_w_q, b_w_o, ffn_norm_w,
           ffn_w_up, ffn_conv_w, ffn_conv_b, ffn_w_down, ple_norm_w, ple_w_gate, ple_w_proj, final_norm_w):
    batch, seq, d = x.shape
    t = batch * seq
    bf = lambda w: w.astype(jnp.bfloat16)
    xf = x.reshape(t, d)
    pos = positions.reshape(t, 1)
    inv_freq = ROPE_THETA ** (-jnp.arange(0, ROT_DIM, 2, dtype=jnp.float32) / ROT_DIM)
    invf = jnp.zeros((1, HEAD_DIM), jnp.float32).at[0, :ROT_DIM].set(jnp.tile(inv_freq, 2))

    kv = None
    for i in range(DEPTH):
        if i < N_A_LAYERS:
            qkv = _norm_proj(xf, attn_norm_w[i], bf(a_w_qkv[i]), name="qkv_proj", out_dtype=jnp.bfloat16,
                             scale=ATTN_SCALE * LOG2_E, scale_tiles=A_WIDTH // 1024, tm=1024, tn=1024)
            o = _stick_attention(qkv, batch, seq)
            xf = _proj_residual([o], bf(a_w_o[i]), xf)
        else:
            if i == N_A_LAYERS:
                kv = _norm_proj(xf, kv_norm_w, bf(b_w_kv), name="kv_proj", out_dtype=jnp.float32,
                                rot_tiles=1, pos=pos, invf=invf, tn=B_WIDTH)
            jb = i - N_A_LAYERS
            q = _norm_proj(xf, attn_norm_w[i], bf(b_w_q[jb]), name="q_proj", out_dtype=jnp.float32,
                           scale=ATTN_SCALE, scale_tiles=1, rot_tiles=1, pos=pos, invf=invf, tn=B_WIDTH)
            o_parts = _band_attention(q, kv, batch, seq)
            xf = _proj_residual(o_parts, bf(b_w_o[jb]), xf)
        xf = _conv_ffn(xf, ffn_norm_w[i], bf(ffn_w_up[i]), ffn_conv_w[i], ffn_conv_b[i], bf(ffn_w_down[i]), seq)
        xf = _ple(xf, ple_norm_w[i], bf(ple_w_gate[i]), p[i].reshape(t, PLE_DIM), bf(ple_w_proj[i]),
                  final_norm_w if i == DEPTH - 1 else None)
    return xf.reshape(batch, seq, d)
```

```python
import functools

import jax
import jax.numpy as jnp
from jax import lax
from jax.experimental import pallas as pl
from jax.experimental.pallas import tpu as pltpu

D_MODEL = 2048
DEPTH = 4
N_A_LAYERS = DEPTH // 2
HEAD_DIM = 128
A_HEADS = D_MODEL // HEAD_DIM
A_WIDTH = A_HEADS * HEAD_DIM
DILATED_GROUPS = ((128, 1), (512, 4), (2048, 16))
N_GROUPS = len(DILATED_GROUPS)
B_HEADS_PER_GROUP = 4
B_WIDTH = N_GROUPS * B_HEADS_PER_GROUP * HEAD_DIM
BAND_BLOCK = 128
ROT_DIM = HEAD_DIM // 4
ROPE_THETA = 500000.0
D_FF = 256 * ((8 * D_MODEL // 3 + 255) // 256)
CONV_WIDTH = 3
PLE_DIM = 256
NORM_EPS = 1e-6
ATTN_SCALE = HEAD_DIM ** -0.5
LOG2_E = 1.4426950408889634
STICK_DONE_LOG2 = 151.0

VMEM_LIMIT_BYTES = 56 * 1024 * 1024
NEG_INF = float("-inf")

_NT_DIMS = (((1,), (1,)), ((), ()))


def _compiler_params(semantics):
    return pltpu.CompilerParams(dimension_semantics=semantics, vmem_limit_bytes=VMEM_LIMIT_BYTES)


def _rms_norm_rows(x, g):
    y = x * lax.rsqrt(jnp.mean(x * x, axis=-1, keepdims=True) + NORM_EPS)
    return y * g


def _layer_spec(block, index_map):
    return pl.BlockSpec((None,) + tuple(block), index_map)


def _rotary_table_kernel(pos_ref, invf_ref, cos_ref, slo_ref, shi_ref):
    ang = pos_ref[...].astype(jnp.float32) * invf_ref[...]
    lane = lax.broadcasted_iota(jnp.int32, ang.shape, 1)
    sin = jnp.sin(ang)
    cos_ref[...] = jnp.cos(ang)
    slo_ref[...] = jnp.where(lane < ROT_DIM // 2, -sin, 0.0)
    shi_ref[...] = jnp.where(lane >= ROT_DIM // 2, sin, 0.0)


def _rotary_tables(pos, invf, *, rows=2048):
    t = pos.shape[0]
    table = jax.ShapeDtypeStruct((t, HEAD_DIM), jnp.float32)
    return pl.pallas_call(
        _rotary_table_kernel,
        grid=(t // rows,),
        in_specs=[pl.BlockSpec((rows, 1), lambda i: (i, 0)), pl.BlockSpec((1, HEAD_DIM), lambda i: (0, 0))],
        out_specs=[pl.BlockSpec((rows, HEAD_DIM), lambda i: (i, 0))] * 3,
        out_shape=[table] * 3,
        compiler_params=_compiler_params(("arbitrary",)),
        name="rotary_tables",
    )(pos, invf)


def _rotary_head(xh, cos, sin_lo, sin_hi):
    return (xh * cos
            + pltpu.roll(xh, HEAD_DIM - ROT_DIM // 2, axis=1) * sin_lo
            + pltpu.roll(xh, ROT_DIM // 2, axis=1) * sin_hi)


def _norm_proj_kernel(*refs, scale, scale_tiles, rot_tiles, tn):
    if rot_tiles:
        x_ref, g_ref, w_ref, cos_ref, slo_ref, shi_ref, o_ref, h_ref = refs
    else:
        x_ref, g_ref, w_ref, o_ref, h_ref = refs
    j = pl.program_id(1)

    @pl.when(j == 0)
    def _():
        h_ref[...] = _rms_norm_rows(x_ref[...], g_ref[...]).astype(h_ref.dtype)

    acc = jnp.dot(h_ref[...], w_ref[...], preferred_element_type=jnp.float32)
    if scale_tiles:
        acc = acc * jnp.where(j < scale_tiles, scale, 1.0)
    if not rot_tiles:
        o_ref[...] = acc.astype(o_ref.dtype)
        return

    @pl.when(j < rot_tiles)
    def _():
        cos, slo, shi = cos_ref[...], slo_ref[...], shi_ref[...]
        for c in range(tn // HEAD_DIM):
            sl = slice(c * HEAD_DIM, (c + 1) * HEAD_DIM)
            o_ref[:, sl] = _rotary_head(acc[:, sl], cos, slo, shi).astype(o_ref.dtype)

    @pl.when(j >= rot_tiles)
    def _():
        o_ref[...] = acc.astype(o_ref.dtype)


def _norm_proj(x, g, g_layer, w, w_layer, *, name, out_dtype, scale=1.0, scale_tiles=0, rot_tiles=0,
               tables=None, tm=512, tn=512):
    t, d = x.shape
    n = w.shape[2]
    in_specs = [
        pl.BlockSpec((tm, d), lambda i, j: (i, 0)),
        _layer_spec((1, d), lambda i, j: (g_layer, 0, 0)),
        _layer_spec((d, tn), lambda i, j: (w_layer, 0, j)),
    ]
    args = [x, g, w]
    if rot_tiles:
        in_specs += [pl.BlockSpec((tm, HEAD_DIM), lambda i, j: (i, 0))] * 3
        args += list(tables)
    return pl.pallas_call(
        functools.partial(_norm_proj_kernel, scale=scale, scale_tiles=scale_tiles, rot_tiles=rot_tiles, tn=tn),
        grid=(t // tm, n // tn),
        in_specs=in_specs,
        out_specs=pl.BlockSpec((tm, tn), lambda i, j: (i, j)),
        out_shape=jax.ShapeDtypeStruct((t, n), out_dtype),
        scratch_shapes=[pltpu.VMEM((tm, d), jnp.bfloat16)],
        compiler_params=_compiler_params(("arbitrary", "arbitrary")),
        name=name,
    )(*args)


def _stick_kernel(q_ref, k_ref, v_ref, u2_ref, o_ref, acc_ref, carry_ref, *, tq, heads):
    qi = pl.program_id(2)
    row = lax.broadcasted_iota(jnp.int32, (tq, tq), 0)
    col = lax.broadcasted_iota(jnp.int32, (tq, tq), 1)
    causal = col < row

    def window(hh, start, n_chunks, diag, carry):
        cols = slice(hh * HEAD_DIM, (hh + 1) * HEAD_DIM)
        rows = pl.ds(pl.multiple_of(start, tq), n_chunks * tq)
        z = lax.dot_general(q_ref[:, cols], k_ref[rows, cols], _NT_DIMS, preferred_element_type=jnp.float32)
        parts = [None] * n_chunks
        for c in reversed(range(n_chunks)):
            zc = z[:, c * tq:(c + 1) * tq]
            sp = jnp.maximum(zc, 0.0) + jnp.log2(1.0 + jnp.exp2(-jnp.abs(zc)))
            masked = diag and c == n_chunks - 1
            if masked:
                sp = jnp.where(causal, sp, 0.0)
            hi = sp.astype(jnp.bfloat16)
            lo = (sp - hi.astype(jnp.float32)).astype(jnp.bfloat16)
            csum = jnp.dot(jnp.concatenate([hi, lo], axis=1), u2_ref[...],
                           preferred_element_type=jnp.float32)
            arg = zc - csum if carry is None else zc - csum - carry
            a = jnp.exp2(arg)
            if masked:
                a = jnp.where(causal, a, 0.0)
            parts[c] = a.astype(jnp.bfloat16)
            carry = csum[:, :1] if carry is None else carry + csum[:, :1]
        a = parts[0] if n_chunks == 1 else jnp.concatenate(parts, axis=1)
        return jnp.dot(a, v_ref[rows, cols], preferred_element_type=jnp.float32), carry

    @pl.when(qi == 0)
    def _():
        for hh in range(heads):
            acc, _ = window(hh, 0, 1, True, None)
            o_ref[:, hh * HEAD_DIM:(hh + 1) * HEAD_DIM] = acc.astype(o_ref.dtype)

    @pl.when(qi > 0)
    def _():
        first = [window(hh, (qi - 1) * tq, 2, True, None) for hh in range(heads)]
        for hh in range(heads):
            acc_ref[hh] = first[hh][0]
            carry_ref[hh] = first[hh][1]

        def more(state):
            j, reach = state
            return (j >= 0) & (reach < STICK_DONE_LOG2)

        for hh in range(heads):
            def step(state, hh=hh):
                j, _ = state
                acc, carry = window(hh, j * tq, 1, False, carry_ref[hh])
                acc_ref[hh] += acc
                carry_ref[hh] = carry
                return j - 1, jnp.min(carry)

            lax.while_loop(more, step, (qi - 2, jnp.min(first[hh][1])))
            o_ref[:, hh * HEAD_DIM:(hh + 1) * HEAD_DIM] = acc_ref[hh].astype(o_ref.dtype)


def _stick_attention(qkv, batch, seq, *, tq=256, heads=4):
    nq = seq // tq
    hw = heads * HEAD_DIM
    groups = A_WIDTH // hw
    tri = jnp.tril(jnp.ones((tq, tq), jnp.bfloat16))
    return pl.pallas_call(
        functools.partial(_stick_kernel, tq=tq, heads=heads),
        grid=(batch, groups, nq),
        in_specs=[
            pl.BlockSpec((tq, hw), lambda b, h, i: (b * nq + i, h)),
            pl.BlockSpec((seq, hw), lambda b, h, i: (b, groups + h)),
            pl.BlockSpec((seq, hw), lambda b, h, i: (b, 2 * groups + h)),
            pl.BlockSpec((2 * tq, tq), lambda b, h, i: (0, 0)),
        ],
        out_specs=pl.BlockSpec((tq, hw), lambda b, h, i: (b * nq + i, h)),
        out_shape=jax.ShapeDtypeStruct((batch * seq, A_WIDTH), jnp.bfloat16),
        scratch_shapes=[pltpu.VMEM((heads, tq, HEAD_DIM), jnp.float32),
                        pltpu.VMEM((heads, tq, 1), jnp.float32)],
        compiler_params=_compiler_params(("arbitrary", "arbitrary", "arbitrary")),
        name="stick_attention",
    )(qkv, qkv, qkv, jnp.concatenate([tri, tri], axis=0))


def _band_kernel(q0, q1, q2, k0, k1, k2, v0, v1, v2, o0, o1, o2, o_scr, l_scr, *, seq):
    q_refs, k_refs, v_refs, o_refs = (q0, q1, q2), (k0, k1, k2), (v0, v1, v2), (o0, o1, o2)
    blk = BAND_BLOCK
    for g, (window, dil) in enumerate(DILATED_GROUPS):
        steps = window // dil
        n_blk = seq // dil // blk
        kw = 2 * blk if n_blk > 1 else blk
        q_ref, k_ref, v_ref = q_refs[g], k_refs[g], v_refs[g]
        qpos = lax.broadcasted_iota(jnp.int32, (blk, kw), 0)
        kpos = lax.broadcasted_iota(jnp.int32, (blk, kw), 1)

        def body(it, _, g=g, dil=dil, steps=steps, n_blk=n_blk, kw=kw, q_ref=q_ref, k_ref=k_ref,
                 v_ref=v_ref, qpos=qpos, kpos=kpos):
            r = it // n_blk
            n = it % n_blk
            k_first = jnp.maximum(n - 1, 0) * blk
            q_rows = pl.ds(r + dil * blk * n, blk, stride=dil)
            k_rows = pl.ds(r + dil * k_first, kw, stride=dil)
            qb = q_ref[q_rows, :].astype(jnp.bfloat16)
            kb = k_ref[k_rows, :].astype(jnp.bfloat16)
            vb = v_ref[k_rows, :].astype(jnp.bfloat16)
            s = lax.dot_general(qb, kb, _NT_DIMS, preferred_element_type=jnp.float32)
            dist = (n * blk - k_first) + qpos - kpos
            s = jnp.where((dist >= 0) & (dist <= steps), s, NEG_INF)
            m = jnp.max(s, axis=-1, keepdims=True)
            e = jnp.exp(s - m)
            den = jnp.sum(e, axis=-1, keepdims=True)
            o = jnp.dot(e.astype(jnp.bfloat16), vb, preferred_element_type=jnp.float32) / den
            o_scr[g, q_rows, :] = o
            l_scr[g, q_rows, :] = jnp.broadcast_to(m + jnp.log(den), (blk, HEAD_DIM))
            return 0

        lax.fori_loop(0, dil * n_blk, body, 0, unroll=8)

    chunk = 256
    def mix(c, _):
        rows = pl.ds(pl.multiple_of(c * chunk, chunk), chunk)
        ls = [l_scr[g, rows, :] for g in range(N_GROUPS)]
        mx = jnp.maximum(jnp.maximum(ls[0], ls[1]), ls[2])
        es = [jnp.exp(l - mx) for l in ls]
        tot = es[0] + es[1] + es[2]
        for g in range(N_GROUPS):
            o_refs[g][rows, :] = (o_scr[g, rows, :] * (es[g] / tot)).astype(o_refs[g].dtype)
        return 0

    lax.fori_loop(0, seq // chunk, mix, 0)


def _band_attention(q, kv, batch, seq):
    hg = B_HEADS_PER_GROUP
    n_heads = N_GROUPS * hg

    def col(base, g):
        return lambda b, h: (b, base + g * hg + h)

    blockspec = lambda base, g: pl.BlockSpec((seq, HEAD_DIM), col(base, g))
    return pl.pallas_call(
        functools.partial(_band_kernel, seq=seq),
        grid=(batch, hg),
        in_specs=([blockspec(0, g) for g in range(N_GROUPS)]
                  + [blockspec(0, g) for g in range(N_GROUPS)]
                  + [blockspec(n_heads, g) for g in range(N_GROUPS)]),
        out_specs=[pl.BlockSpec((seq, HEAD_DIM), lambda b, h: (b, h))] * N_GROUPS,
        out_shape=[jax.ShapeDtypeStruct((batch * seq, hg * HEAD_DIM), jnp.bfloat16)] * N_GROUPS,
        scratch_shapes=[pltpu.VMEM((N_GROUPS, seq, HEAD_DIM), jnp.float32)] * 2,
        compiler_params=_compiler_params(("arbitrary", "arbitrary")),
        name="band_attention",
    )(q, q, q, kv, kv, kv, kv, kv, kv)


def _proj_residual_kernel(*refs, n_parts):
    a_refs, w_ref, x_ref, o_ref = refs[:n_parts], refs[n_parts], refs[n_parts + 1], refs[n_parts + 2]
    acc = x_ref[...]
    gw = w_ref.shape[0] // n_parts
    for g in range(n_parts):
        acc = acc + jnp.dot(a_refs[g][...], w_ref[g * gw:(g + 1) * gw, :], preferred_element_type=jnp.float32)
    o_ref[...] = acc


def _proj_residual(a_parts, w, layer, x, *, tm=512):
    t, d = x.shape
    kdim = w.shape[1]
    n_parts = len(a_parts)
    return pl.pallas_call(
        functools.partial(_proj_residual_kernel, n_parts=n_parts),
        grid=(t // tm,),
        in_specs=([pl.BlockSpec((tm, kdim // n_parts), lambda i: (i, 0))] * n_parts
                  + [_layer_spec((kdim, d), lambda i: (layer, 0, 0)),
                     pl.BlockSpec((tm, d), lambda i: (i, 0))]),
        out_specs=pl.BlockSpec((tm, d), lambda i: (i, 0)),
        out_shape=jax.ShapeDtypeStruct((t, d), jnp.float32),
        compiler_params=_compiler_params(("arbitrary",)),
        name="proj_residual",
    )(*a_parts, w, x)


def _conv_ffn_kernel(x_ref, g_ref, wg_ref, wv_ref, cwg_ref, cwv_ref, cbg_ref, cbv_ref, wd_ref,
                     o_ref, h_ref, acta_ref, actb_ref, tail_ref, *, tm, nj, tiles_per_seq):
    i = pl.program_id(0)
    j = pl.program_id(1)
    act_refs = (acta_ref, actb_ref)
    rid = lax.broadcasted_iota(jnp.int32, (tm, 1), 0)

    def up(act_ref):
        def conv_half(w_ref, cw_ref, cb_ref, slot):
            u = jnp.dot(h_ref[...], w_ref[...], preferred_element_type=jnp.float32)
            prev = tail_ref[slot, j]
            p1 = prev[7:8, :]
            p2 = prev[6:7, :]
            um1 = jnp.where(rid == 0, p1, pltpu.roll(u, 1, axis=0))
            um2 = jnp.where(rid == 0, p2, jnp.where(rid == 1, p1, pltpu.roll(u, 2, axis=0)))
            tail_ref[slot, j] = u[tm - 8:, :]
            cw = cw_ref[...]
            return cw[0:1, :] * um2 + cw[1:2, :] * um1 + cw[2:3, :] * u + cb_ref[...]

        gate = conv_half(wg_ref, cwg_ref, cbg_ref, 0)
        val = conv_half(wv_ref, cwv_ref, cbv_ref, 1)
        act_ref[...] = (gate * (1.0 / (1.0 + jnp.exp(-gate))) * val).astype(act_ref.dtype)

    def down(act_ref):
        o_ref[...] += jnp.dot(act_ref[...], wd_ref[...], preferred_element_type=jnp.float32)

    @pl.when(j == 0)
    def _():
        x = x_ref[...]
        h_ref[...] = _rms_norm_rows(x, g_ref[...]).astype(h_ref.dtype)
        o_ref[...] = x

    @pl.when((j == 0) & ((i % tiles_per_seq) == 0))
    def _():
        tail_ref[...] = jnp.zeros(tail_ref.shape, tail_ref.dtype)

    @pl.when(j == 0)
    def _():
        up(act_refs[0])

    for parity in range(2):
        @pl.when((j > 0) & (j < nj) & (j % 2 == parity))
        def _(parity=parity):
            up(act_refs[parity])
            down(act_refs[1 - parity])

    @pl.when(j == nj)
    def _():
        down(act_refs[(nj - 1) % 2])


def _conv_ffn(x, g, w_up, conv_w, conv_b, w_down, layer, seq, *, tm=1024, tf=512):
    t, d = x.shape
    nj = D_FF // tf

    def up_tile(j):
        return jnp.minimum(j, nj - 1)

    def down_tile(j):
        return jnp.maximum(j - 1, 0)

    return pl.pallas_call(
        functools.partial(_conv_ffn_kernel, tm=tm, nj=nj, tiles_per_seq=seq // tm),
        grid=(t // tm, nj + 1),
        in_specs=[
            pl.BlockSpec((tm, d), lambda i, j: (i, 0)),
            _layer_spec((1, d), lambda i, j: (layer, 0, 0)),
            _layer_spec((d, tf), lambda i, j: (layer, 0, up_tile(j))),
            _layer_spec((d, tf), lambda i, j: (layer, 0, nj + up_tile(j))),
            _layer_spec((CONV_WIDTH, tf), lambda i, j: (layer, 0, up_tile(j))),
            _layer_spec((CONV_WIDTH, tf), lambda i, j: (layer, 0, nj + up_tile(j))),
            _layer_spec((1, tf), lambda i, j: (layer, 0, up_tile(j))),
            _layer_spec((1, tf), lambda i, j: (layer, 0, nj + up_tile(j))),
            _layer_spec((tf, d), lambda i, j: (layer, down_tile(j), 0)),
        ],
        out_specs=pl.BlockSpec((tm, d), lambda i, j: (i, 0)),
        out_shape=jax.ShapeDtypeStruct((t, d), jnp.float32),
        scratch_shapes=[pltpu.VMEM((tm, d), jnp.bfloat16),
                        pltpu.VMEM((tm, tf), jnp.bfloat16),
                        pltpu.VMEM((tm, tf), jnp.bfloat16),
                        pltpu.VMEM((2, nj, 8, tf), jnp.float32)],
        compiler_params=_compiler_params(("arbitrary", "arbitrary")),
        name="conv_ffn",
    )(x, g, w_up, w_up, conv_w, conv_w, conv_b, conv_b, w_down)


def _ple_kernel(*refs, final_norm):
    if final_norm:
        x_ref, g_ref, wg_ref, p_ref, wp_ref, fg_ref, o_ref = refs
    else:
        x_ref, g_ref, wg_ref, p_ref, wp_ref, o_ref = refs
    x = x_ref[...]
    h = _rms_norm_rows(x, g_ref[...]).astype(jnp.bfloat16)
    gate = jnp.dot(h, wg_ref[...], preferred_element_type=jnp.float32)
    gate = 1.0 / (1.0 + jnp.exp(-gate))
    proj = jnp.dot(p_ref[...].astype(jnp.bfloat16), wp_ref[...], preferred_element_type=jnp.float32)
    y = x + gate * proj
    if final_norm:
        y = _rms_norm_rows(y, fg_ref[...])
    o_ref[...] = y


def _ple(x, g, w_gate, p, w_proj, layer, final_g=None, *, tm=256):
    t, d = x.shape
    final_norm = final_g is not None
    in_specs = [
        pl.BlockSpec((tm, d), lambda i: (i, 0)),
        _layer_spec((1, d), lambda i: (layer, 0, 0)),
        _layer_spec((d, d), lambda i: (layer, 0, 0)),
        _layer_spec((tm, PLE_DIM), lambda i: (layer, i, 0)),
        _layer_spec((PLE_DIM, d), lambda i: (layer, 0, 0)),
    ]
    args = [x, g, w_gate, p, w_proj]
    if final_norm:
        in_specs.append(pl.BlockSpec((1, d), lambda i: (0, 0)))
        args.append(final_g.reshape(1, d))
    return pl.pallas_call(
        functools.partial(_ple_kernel, final_norm=final_norm),
        grid=(t // tm,),
        in_specs=in_specs,
        out_specs=pl.BlockSpec((tm, d), lambda i: (i, 0)),
        out_shape=jax.ShapeDtypeStruct((t, d), jnp.float32),
        compiler_params=_compiler_params(("arbitrary",)),
        name="ple",
    )(*args)


def kernel(x, p, positions, attn_norm_w, a_w_qkv, a_w_o, kv_norm_w, b_w_kv, b_w_q, b_w_o, ffn_norm_w,
           ffn_w_up, ffn_conv_w, ffn_conv_b, ffn_w_down, ple_norm_w, ple_w_gate, ple_w_proj, final_norm_w):
    batch, seq, d = x.shape
    t = batch * seq
    bf = lambda w: w.astype(jnp.bfloat16)
    rows = lambda w: w.reshape(w.shape[0], 1, w.shape[1])
    xf = x.reshape(t, d)
    pf = p.reshape(DEPTH, t, PLE_DIM)

    inv_freq = ROPE_THETA ** (-jnp.arange(0, ROT_DIM, 2, dtype=jnp.float32) / ROT_DIM)
    invf = jnp.zeros((1, HEAD_DIM), jnp.float32).at[0, :ROT_DIM].set(jnp.tile(inv_freq, 2))
    tables = _rotary_tables(positions.reshape(t, 1), invf)

    attn_g, ffn_g, ple_g = rows(attn_norm_w), rows(ffn_norm_w), rows(ple_norm_w)
    kv_g = kv_norm_w.reshape(1, 1, d)
    w_qkv, w_ao = bf(a_w_qkv), bf(a_w_o)
    w_kv, w_q, w_bo = bf(b_w_kv)[None], bf(b_w_q), bf(b_w_o)
    w_up, w_down = bf(ffn_w_up), bf(ffn_w_down)
    conv_b = rows(ffn_conv_b)
    w_pg, w_pp = bf(ple_w_gate), bf(ple_w_proj)

    kv = None
    for i in range(DEPTH):
        if i < N_A_LAYERS:
            qkv = _norm_proj(xf, attn_g, i, w_qkv, i, name="qkv_proj", out_dtype=jnp.bfloat16,
                             scale=ATTN_SCALE * LOG2_E, scale_tiles=A_WIDTH // 1024, tm=1024, tn=1024)
            o = _stick_attention(qkv, batch, seq)
            xf = _proj_residual([o], w_ao, i, xf)
        else:
            if i == N_A_LAYERS:
                kv = _norm_proj(xf, kv_g, 0, w_kv, 0, name="kv_proj", out_dtype=jnp.float32,
                                rot_tiles=1, tables=tables, tn=B_WIDTH)
            jb = i - N_A_LAYERS
            q = _norm_proj(xf, attn_g, i, w_q, jb, name="q_proj", out_dtype=jnp.float32,
                           scale=ATTN_SCALE, scale_tiles=1, rot_tiles=1, tables=tables, tn=B_WIDTH)
            o_parts = _band_attention(q, kv, batch, seq)
            xf = _proj_residual(o_parts, w_bo, jb, xf)
        xf = _conv_ffn(xf, ffn_g, w_up, ffn_conv_w, conv_b, w_down, i, seq)
        xf = _ple(xf, ple_g, w_pg, pf, w_pp, i, final_norm_w if i == DEPTH - 1 else None)
    return xf.reshape(batch, seq, d)
```

```python
import functools

import jax
import jax.numpy as jnp
from jax import lax
from jax.experimental import pallas as pl
from jax.experimental.pallas import tpu as pltpu

D_MODEL = 2048
DEPTH = 4
N_A_LAYERS = DEPTH // 2
HEAD_DIM = 128
A_HEADS = D_MODEL // HEAD_DIM
A_WIDTH = A_HEADS * HEAD_DIM
DILATED_GROUPS = ((128, 1), (512, 4), (2048, 16))
N_GROUPS = len(DILATED_GROUPS)
B_HEADS_PER_GROUP = 4
B_WIDTH = N_GROUPS * B_HEADS_PER_GROUP * HEAD_DIM
BAND_BLOCK = 128
BAND_BATCH = 8
ROT_DIM = HEAD_DIM // 4
ROPE_THETA = 500000.0
D_FF = 256 * ((8 * D_MODEL // 3 + 255) // 256)
CONV_WIDTH = 3
PLE_DIM = 256
NORM_EPS = 1e-6
ATTN_SCALE = HEAD_DIM ** -0.5
LOG2_E = 1.4426950408889634
STICK_DONE_LOG2 = 151.0

VMEM_LIMIT_BYTES = 56 * 1024 * 1024
NEG_INF = float("-inf")

_NT_DIMS = (((1,), (1,)), ((), ()))


def _compiler_params(semantics):
    return pltpu.CompilerParams(dimension_semantics=semantics, vmem_limit_bytes=VMEM_LIMIT_BYTES)


def _rms_norm_rows(x, g):
    y = x * lax.rsqrt(jnp.mean(x * x, axis=-1, keepdims=True) + NORM_EPS)
    return y * g


def _layer_spec(block, index_map):
    return pl.BlockSpec((None,) + tuple(block), index_map)


def _rotary_table_kernel(pos_ref, invf_ref, cos_ref, slo_ref, shi_ref):
    ang = pos_ref[...].astype(jnp.float32) * invf_ref[...]
    lane = lax.broadcasted_iota(jnp.int32, ang.shape, 1)
    sin = jnp.sin(ang)
    cos_ref[...] = jnp.cos(ang)
    slo_ref[...] = jnp.where(lane < ROT_DIM // 2, -sin, 0.0)
    shi_ref[...] = jnp.where(lane >= ROT_DIM // 2, sin, 0.0)


def _rotary_tables(pos, invf, *, rows=2048):
    t = pos.shape[0]
    table = jax.ShapeDtypeStruct((t, HEAD_DIM), jnp.float32)
    return pl.pallas_call(
        _rotary_table_kernel,
        grid=(t // rows,),
        in_specs=[pl.BlockSpec((rows, 1), lambda i: (i, 0)), pl.BlockSpec((1, HEAD_DIM), lambda i: (0, 0))],
        out_specs=[pl.BlockSpec((rows, HEAD_DIM), lambda i: (i, 0))] * 3,
        out_shape=[table] * 3,
        compiler_params=_compiler_params(("arbitrary",)),
        name="rotary_tables",
    )(pos, invf)


def _rotary_head(xh, cos, sin_lo, sin_hi):
    return (xh * cos
            + pltpu.roll(xh, HEAD_DIM - ROT_DIM // 2, axis=1) * sin_lo
            + pltpu.roll(xh, ROT_DIM // 2, axis=1) * sin_hi)


def _norm_proj_kernel(*refs, scale, scale_tiles, rot_tiles, tn):
    if rot_tiles:
        x_ref, g_ref, w_ref, cos_ref, slo_ref, shi_ref, o_ref, h_ref = refs
    else:
        x_ref, g_ref, w_ref, o_ref, h_ref = refs
    j = pl.program_id(1)

    @pl.when(j == 0)
    def _():
        h_ref[...] = _rms_norm_rows(x_ref[...], g_ref[...]).astype(h_ref.dtype)

    acc = jnp.dot(h_ref[...], w_ref[...], preferred_element_type=jnp.float32)
    if scale_tiles:
        acc = acc * jnp.where(j < scale_tiles, scale, 1.0)
    if not rot_tiles:
        o_ref[...] = acc.astype(o_ref.dtype)
        return

    @pl.when(j < rot_tiles)
    def _():
        cos, slo, shi = cos_ref[...], slo_ref[...], shi_ref[...]
        for c in range(tn // HEAD_DIM):
            sl = slice(c * HEAD_DIM, (c + 1) * HEAD_DIM)
            o_ref[:, sl] = _rotary_head(acc[:, sl], cos, slo, shi).astype(o_ref.dtype)

    @pl.when(j >= rot_tiles)
    def _():
        o_ref[...] = acc.astype(o_ref.dtype)


def _norm_proj(x, g, g_layer, w, w_layer, *, name, out_dtype, scale=1.0, scale_tiles=0, rot_tiles=0,
               tables=None, tm=512, tn=512):
    t, d = x.shape
    n = w.shape[2]
    in_specs = [
        pl.BlockSpec((tm, d), lambda i, j: (i, 0)),
        _layer_spec((1, d), lambda i, j: (g_layer, 0, 0)),
        _layer_spec((d, tn), lambda i, j: (w_layer, 0, j)),
    ]
    args = [x, g, w]
    if rot_tiles:
        in_specs += [pl.BlockSpec((tm, HEAD_DIM), lambda i, j: (i, 0))] * 3
        args += list(tables)
    return pl.pallas_call(
        functools.partial(_norm_proj_kernel, scale=scale, scale_tiles=scale_tiles, rot_tiles=rot_tiles, tn=tn),
        grid=(t // tm, n // tn),
        in_specs=in_specs,
        out_specs=pl.BlockSpec((tm, tn), lambda i, j: (i, j)),
        out_shape=jax.ShapeDtypeStruct((t, n), out_dtype),
        scratch_shapes=[pltpu.VMEM((tm, d), jnp.bfloat16)],
        compiler_params=_compiler_params(("arbitrary", "arbitrary")),
        name=name,
    )(*args)


def _stick_kernel(q_ref, k_ref, v_ref, u2_ref, o_ref, acc_ref, carry_ref, *, tq, heads):
    qi = pl.program_id(2)
    row = lax.broadcasted_iota(jnp.int32, (tq, tq), 0)
    col = lax.broadcasted_iota(jnp.int32, (tq, tq), 1)
    causal = col < row

    def windows(start, n_chunks, diag, carries):
        rows = pl.ds(pl.multiple_of(start, tq), n_chunks * tq)
        hcols = [slice(hh * HEAD_DIM, (hh + 1) * HEAD_DIM) for hh in range(heads)]
        chunks = [(hh, c) for hh in range(heads) for c in reversed(range(n_chunks))]
        is_diag = lambda c: diag and c == n_chunks - 1
        zs = [lax.dot_general(q_ref[:, hc], k_ref[rows, hc], _NT_DIMS, preferred_element_type=jnp.float32)
              for hc in hcols]
        zc = {(hh, c): zs[hh][:, c * tq:(c + 1) * tq] for hh, c in chunks}
        split = {}
        for hh, c in chunks:
            z = zc[hh, c]
            sp = jnp.maximum(z, 0.0) + jnp.log2(1.0 + jnp.exp2(-jnp.abs(z)))
            if is_diag(c):
                sp = jnp.where(causal, sp, 0.0)
            hi = sp.astype(jnp.bfloat16)
            lo = (sp - hi.astype(jnp.float32)).astype(jnp.bfloat16)
            split[hh, c] = jnp.concatenate([hi, lo], axis=1)
        csum = {hc: jnp.dot(split[hc], u2_ref[...], preferred_element_type=jnp.float32) for hc in chunks}
        carries = list(carries) if carries is not None else [None] * heads
        parts = {}
        for hh, c in chunks:
            arg = zc[hh, c] - csum[hh, c]
            if carries[hh] is not None:
                arg = arg - carries[hh]
            a = jnp.exp2(arg)
            if is_diag(c):
                a = jnp.where(causal, a, 0.0)
            parts[hh, c] = a.astype(jnp.bfloat16)
            total = csum[hh, c][:, :1]
            carries[hh] = total if carries[hh] is None else carries[hh] + total
        accs = []
        for hh in range(heads):
            a = [parts[hh, c] for c in range(n_chunks)]
            a = a[0] if n_chunks == 1 else jnp.concatenate(a, axis=1)
            accs.append(jnp.dot(a, v_ref[rows, hcols[hh]], preferred_element_type=jnp.float32))
        return accs, carries

    def store_out(accs):
        for hh in range(heads):
            o_ref[:, hh * HEAD_DIM:(hh + 1) * HEAD_DIM] = accs[hh].astype(o_ref.dtype)

    @pl.when(qi == 0)
    def _():
        store_out(windows(0, 1, True, None)[0])

    @pl.when(qi > 0)
    def _():
        accs, carries = windows((qi - 1) * tq, 2, True, None)
        store_out(accs)

        def reach_of(carries):
            return jnp.min(functools.reduce(jnp.minimum, carries))

        def more(state):
            j, reach = state
            return (j >= 0) & (reach < STICK_DONE_LOG2)

        reach0 = reach_of(carries)

        @pl.when(more((qi - 2, reach0)))
        def _():
            for hh in range(heads):
                acc_ref[hh] = accs[hh]
                carry_ref[hh] = carries[hh]

            def step(state):
                j, _ = state
                more_accs, new_carries = windows(j * tq, 1, False, [carry_ref[hh] for hh in range(heads)])
                for hh in range(heads):
                    acc_ref[hh] += more_accs[hh]
                    carry_ref[hh] = new_carries[hh]
                return j - 1, reach_of(new_carries)

            lax.while_loop(more, step, (qi - 2, reach0))
            store_out([acc_ref[hh] for hh in range(heads)])


def _stick_attention(qkv, batch, seq, *, tq=256, heads=8):
    nq = seq // tq
    hw = heads * HEAD_DIM
    groups = A_WIDTH // hw
    tri = jnp.tril(jnp.ones((tq, tq), jnp.bfloat16))
    return pl.pallas_call(
        functools.partial(_stick_kernel, tq=tq, heads=heads),
        grid=(batch, groups, nq),
        in_specs=[
            pl.BlockSpec((tq, hw), lambda b, h, i: (b * nq + i, h)),
            pl.BlockSpec((seq, hw), lambda b, h, i: (b, groups + h)),
            pl.BlockSpec((seq, hw), lambda b, h, i: (b, 2 * groups + h)),
            pl.BlockSpec((2 * tq, tq), lambda b, h, i: (0, 0)),
        ],
        out_specs=pl.BlockSpec((tq, hw), lambda b, h, i: (b * nq + i, h)),
        out_shape=jax.ShapeDtypeStruct((batch * seq, A_WIDTH), jnp.bfloat16),
        scratch_shapes=[pltpu.VMEM((heads, tq, HEAD_DIM), jnp.float32),
                        pltpu.VMEM((heads, tq, 1), jnp.float32)],
        compiler_params=_compiler_params(("arbitrary", "arbitrary", "arbitrary")),
        name="stick_attention",
    )(qkv, qkv, qkv, jnp.concatenate([tri, tri], axis=0))


def _band_kernel(q0, q1, q2, k0, k1, k2, v0, v1, v2, o0, o1, o2, o_scr, l_scr, *, seq):
    q_refs, k_refs, v_refs, o_refs = (q0, q1, q2), (k0, k1, k2), (v0, v1, v2), (o0, o1, o2)
    blk = BAND_BLOCK
    for g, (window, dil) in enumerate(DILATED_GROUPS):
        steps = window // dil
        n_blk = seq // dil // blk
        kw = 2 * blk if n_blk > 1 else blk
        q_ref, k_ref, v_ref = q_refs[g], k_refs[g], v_refs[g]
        qpos = lax.broadcasted_iota(jnp.int32, (blk, kw), 0)
        kpos = lax.broadcasted_iota(jnp.int32, (blk, kw), 1)

        def body(batch_idx, _, g=g, dil=dil, steps=steps, n_blk=n_blk, kw=kw, q_ref=q_ref, k_ref=k_ref,
                 v_ref=v_ref, qpos=qpos, kpos=kpos):
            pairs = []
            for u in range(BAND_BATCH):
                it = batch_idx * BAND_BATCH + u
                r = it // n_blk
                n = it % n_blk
                k_first = jnp.maximum(n - 1, 0) * blk
                q_rows = pl.ds(r + dil * blk * n, blk, stride=dil)
                k_rows = pl.ds(r + dil * k_first, kw, stride=dil)
                pairs.append((n * blk - k_first, q_rows, k_rows))
            scores = [lax.dot_general(q_ref[q_rows, :].astype(jnp.bfloat16),
                                      k_ref[k_rows, :].astype(jnp.bfloat16), _NT_DIMS,
                                      preferred_element_type=jnp.float32)
                      for _, q_rows, k_rows in pairs]
            soft = []
            for (offset, _, _), s in zip(pairs, scores):
                dist = offset + qpos - kpos
                s = jnp.where((dist >= 0) & (dist <= steps), s, NEG_INF)
                m = jnp.max(s, axis=-1, keepdims=True)
                e = jnp.exp(s - m)
                den = jnp.sum(e, axis=-1, keepdims=True)
                soft.append((e.astype(jnp.bfloat16), den, m + jnp.log(den)))
            outs = [jnp.dot(e, v_ref[k_rows, :].astype(jnp.bfloat16), preferred_element_type=jnp.float32)
                    for (e, _, _), (_, _, k_rows) in zip(soft, pairs)]
            for (_, den, lse), (_, q_rows, _), o in zip(soft, pairs, outs):
                o_scr[g, q_rows, :] = o / den
                l_scr[g, q_rows, :] = jnp.broadcast_to(lse, (blk, HEAD_DIM))
            return 0

        lax.fori_loop(0, dil * n_blk // BAND_BATCH, body, 0)

    chunk = 256
    def mix(c, _):
        rows = pl.ds(pl.multiple_of(c * chunk, chunk), chunk)
        ls = [l_scr[g, rows, :] for g in range(N_GROUPS)]
        mx = jnp.maximum(jnp.maximum(ls[0], ls[1]), ls[2])
        es = [jnp.exp(l - mx) for l in ls]
        tot = es[0] + es[1] + es[2]
        for g in range(N_GROUPS):
            o_refs[g][rows, :] = (o_scr[g, rows, :] * (es[g] / tot)).astype(o_refs[g].dtype)
        return 0

    lax.fori_loop(0, seq // chunk, mix, 0)


def _band_attention(q, kv, batch, seq):
    hg = B_HEADS_PER_GROUP
    n_heads = N_GROUPS * hg

    def col(base, g):
        return lambda b, h: (b, base + g * hg + h)

    blockspec = lambda base, g: pl.BlockSpec((seq, HEAD_DIM), col(base, g))
    return pl.pallas_call(
        functools.partial(_band_kernel, seq=seq),
        grid=(batch, hg),
        in_specs=([blockspec(0, g) for g in range(N_GROUPS)]
                  + [blockspec(0, g) for g in range(N_GROUPS)]
                  + [blockspec(n_heads, g) for g in range(N_GROUPS)]),
        out_specs=[pl.BlockSpec((seq, HEAD_DIM), lambda b, h: (b, h))] * N_GROUPS,
        out_shape=[jax.ShapeDtypeStruct((batch * seq, hg * HEAD_DIM), jnp.bfloat16)] * N_GROUPS,
        scratch_shapes=[pltpu.VMEM((N_GROUPS, seq, HEAD_DIM), jnp.float32)] * 2,
        compiler_params=_compiler_params(("arbitrary", "arbitrary")),
        name="band_attention",
    )(q, q, q, kv, kv, kv, kv, kv, kv)


def _proj_residual_kernel(*refs, n_parts):
    a_refs, w_ref, x_ref, o_ref = refs[:n_parts], refs[n_parts], refs[n_parts + 1], refs[n_parts + 2]
    acc = x_ref[...]
    gw = w_ref.shape[0] // n_parts
    for g in range(n_parts):
        acc = acc + jnp.dot(a_refs[g][...], w_ref[g * gw:(g + 1) * gw, :], preferred_element_type=jnp.float32)
    o_ref[...] = acc


def _proj_residual(a_parts, w, layer, x, *, tm=512):
    t, d = x.shape
    kdim = w.shape[1]
    n_parts = len(a_parts)
    return pl.pallas_call(
        functools.partial(_proj_residual_kernel, n_parts=n_parts),
        grid=(t // tm,),
        in_specs=([pl.BlockSpec((tm, kdim // n_parts), lambda i: (i, 0))] * n_parts
                  + [_layer_spec((kdim, d), lambda i: (layer, 0, 0)),
                     pl.BlockSpec((tm, d), lambda i: (i, 0))]),
        out_specs=pl.BlockSpec((tm, d), lambda i: (i, 0)),
        out_shape=jax.ShapeDtypeStruct((t, d), jnp.float32),
        compiler_params=_compiler_params(("arbitrary",)),
        name="proj_residual",
    )(*a_parts, w, x)


def _conv_ffn_kernel(x_ref, g_ref, wg_ref, wv_ref, cwg_ref, cwv_ref, cbg_ref, cbv_ref, wd_ref,
                     o_ref, h_ref, acta_ref, actb_ref, tail_ref, *, tm, nj, tiles_per_seq):
    i = pl.program_id(0)
    j = pl.program_id(1)
    act_refs = (acta_ref, actb_ref)
    rid = lax.broadcasted_iota(jnp.int32, (tm, 1), 0)

    def up(act_ref):
        def conv_half(w_ref, cw_ref, cb_ref, slot):
            u = jnp.dot(h_ref[...], w_ref[...], preferred_element_type=jnp.float32)
            prev = tail_ref[slot, j]
            p1 = prev[7:8, :]
            p2 = prev[6:7, :]
            um1 = jnp.where(rid == 0, p1, pltpu.roll(u, 1, axis=0))
            um2 = jnp.where(rid == 0, p2, jnp.where(rid == 1, p1, pltpu.roll(u, 2, axis=0)))
            tail_ref[slot, j] = u[tm - 8:, :]
            cw = cw_ref[...]
            return cw[0:1, :] * um2 + cw[1:2, :] * um1 + cw[2:3, :] * u + cb_ref[...]

        gate = conv_half(wg_ref, cwg_ref, cbg_ref, 0)
        val = conv_half(wv_ref, cwv_ref, cbv_ref, 1)
        act_ref[...] = (gate * (1.0 / (1.0 + jnp.exp(-gate))) * val).astype(act_ref.dtype)

    def down(act_ref):
        o_ref[...] += jnp.dot(act_ref[...], wd_ref[...], preferred_element_type=jnp.float32)

    @pl.when(j == 0)
    def _():
        x = x_ref[...]
        h_ref[...] = _rms_norm_rows(x, g_ref[...]).astype(h_ref.dtype)
        o_ref[...] = x

    @pl.when((j == 0) & ((i % tiles_per_seq) == 0))
    def _():
        tail_ref[...] = jnp.zeros(tail_ref.shape, tail_ref.dtype)

    @pl.when(j == 0)
    def _():
        up(act_refs[0])

    for parity in range(2):
        @pl.when((j > 0) & (j < nj) & (j % 2 == parity))
        def _(parity=parity):
            up(act_refs[parity])
            down(act_refs[1 - parity])

    @pl.when(j == nj)
    def _():
        down(act_refs[(nj - 1) % 2])


def _conv_ffn(x, g, w_up, conv_w, conv_b, w_down, layer, seq, *, tm=1024, tf=512):
    t, d = x.shape
    nj = D_FF // tf

    def up_tile(j):
        return jnp.minimum(j, nj - 1)

    def down_tile(j):
        return jnp.maximum(j - 1, 0)

    return pl.pallas_call(
        functools.partial(_conv_ffn_kernel, tm=tm, nj=nj, tiles_per_seq=seq // tm),
        grid=(t // tm, nj + 1),
        in_specs=[
            pl.BlockSpec((tm, d), lambda i, j: (i, 0)),
            _layer_spec((1, d), lambda i, j: (layer, 0, 0)),
            _layer_spec((d, tf), lambda i, j: (layer, 0, up_tile(j))),
            _layer_spec((d, tf), lambda i, j: (layer, 0, nj + up_tile(j))),
            _layer_spec((CONV_WIDTH, tf), lambda i, j: (layer, 0, up_tile(j))),
            _layer_spec((CONV_WIDTH, tf), lambda i, j: (layer, 0, nj + up_tile(j))),
            _layer_spec((1, tf), lambda i, j: (layer, 0, up_tile(j))),
            _layer_spec((1, tf), lambda i, j: (layer, 0, nj + up_tile(j))),
            _layer_spec((tf, d), lambda i, j: (layer, down_tile(j), 0)),
        ],
        out_specs=pl.BlockSpec((tm, d), lambda i, j: (i, 0)),
        out_shape=jax.ShapeDtypeStruct((t, d), jnp.float32),
        scratch_shapes=[pltpu.VMEM((tm, d), jnp.bfloat16),
                        pltpu.VMEM((tm, tf), jnp.bfloat16),
                        pltpu.VMEM((tm, tf), jnp.bfloat16),
                        pltpu.VMEM((2, nj, 8, tf), jnp.float32)],
        compiler_params=_compiler_params(("arbitrary", "arbitrary")),
        name="conv_ffn",
    )(x, g, w_up, w_up, conv_w, conv_w, conv_b, conv_b, w_down)


def _ple_kernel(*refs, final_norm):
    if final_norm:
        x_ref, g_ref, wg_ref, p_ref, wp_ref, fg_ref, o_ref = refs
    else:
        x_ref, g_ref, wg_ref, p_ref, wp_ref, o_ref = refs
    x = x_ref[...]
    h = _rms_norm_rows(x, g_ref[...]).astype(jnp.bfloat16)
    gate = jnp.dot(h, wg_ref[...], preferred_element_type=jnp.float32)
    gate = 1.0 / (1.0 + jnp.exp(-gate))
    proj = jnp.dot(p_ref[...].astype(jnp.bfloat16), wp_ref[...], preferred_element_type=jnp.float32)
    y = x + gate * proj
    if final_norm:
        y = _rms_norm_rows(y, fg_ref[...])
    o_ref[...] = y


def _ple(x, g, w_gate, p, w_proj, layer, final_g=None, *, tm=256):
    t, d = x.shape
    final_norm = final_g is not None
    in_specs = [
        pl.BlockSpec((tm, d), lambda i: (i, 0)),
        _layer_spec((1, d), lambda i: (layer, 0, 0)),
        _layer_spec((d, d), lambda i: (layer, 0, 0)),
        _layer_spec((tm, PLE_DIM), lambda i: (layer, i, 0)),
        _layer_spec((PLE_DIM, d), lambda i: (layer, 0, 0)),
    ]
    args = [x, g, w_gate, p, w_proj]
    if final_norm:
        in_specs.append(pl.BlockSpec((1, d), lambda i: (0, 0)))
        args.append(final_g.reshape(1, d))
    return pl.pallas_call(
        functools.partial(_ple_kernel, final_norm=final_norm),
        grid=(t // tm,),
        in_specs=in_specs,
        out_specs=pl.BlockSpec((tm, d), lambda i: (i, 0)),
        out_shape=jax.ShapeDtypeStruct((t, d), jnp.float32),
        compiler_params=_compiler_params(("arbitrary",)),
        name="ple",
    )(*args)


def kernel(x, p, positions, attn_norm_w, a_w_qkv, a_w_o, kv_norm_w, b_w_kv, b_w_q, b_w_o, ffn_norm_w,
           ffn_w_up, ffn_conv_w, ffn_conv_b, ffn_w_down, ple_norm_w, ple_w_gate, ple_w_proj, final_norm_w):
    batch, seq, d = x.shape
    t = batch * seq
    bf = lambda w: w.astype(jnp.bfloat16)
    rows = lambda w: w.reshape(w.shape[0], 1, w.shape[1])
    xf = x.reshape(t, d)
    pf = p.reshape(DEPTH, t, PLE_DIM)

    inv_freq = ROPE_THETA ** (-jnp.arange(0, ROT_DIM, 2, dtype=jnp.float32) / ROT_DIM)
    invf = jnp.zeros((1, HEAD_DIM), jnp.float32).at[0, :ROT_DIM].set(jnp.tile(inv_freq, 2))
    tables = _rotary_tables(positions.reshape(t, 1), invf)

    attn_g, ffn_g, ple_g = rows(attn_norm_w), rows(ffn_norm_w), rows(ple_norm_w)
    kv_g = kv_norm_w.reshape(1, 1, d)
    w_qkv, w_ao = bf(a_w_qkv), bf(a_w_o)
    w_kv, w_q, w_bo = bf(b_w_kv)[None], bf(b_w_q), bf(b_w_o)
    w_up, w_down = bf(ffn_w_up), bf(ffn_w_down)
    conv_b = rows(ffn_conv_b)
    w_pg, w_pp = bf(ple_w_gate), bf(ple_w_proj)

    kv = None
    for i in range(DEPTH):
        if i < N_A_LAYERS:
            qkv = _norm_proj(xf, attn_g, i, w_qkv, i, name="qkv_proj", out_dtype=jnp.bfloat16,
                             scale=ATTN_SCALE * LOG2_E, scale_tiles=A_WIDTH // 1024, tm=1024, tn=1024)
            o = _stick_attention(qkv, batch, seq)
            xf = _proj_residual([o], w_ao, i, xf)
        else:
            if i == N_A_LAYERS:
                kv = _norm_proj(xf, kv_g, 0, w_kv, 0, name="kv_proj", out_dtype=jnp.float32,
                                rot_tiles=1, tables=tables, tn=B_WIDTH)
            jb = i - N_A_LAYERS
            q = _norm_proj(xf, attn_g, i, w_q, jb, name="q_proj", out_dtype=jnp.float32,
                           scale=ATTN_SCALE, scale_tiles=1, rot_tiles=1, tables=tables, tn=B_WIDTH)
            o_parts = _band_attention(q, kv, batch, seq)
            xf = _proj_residual(o_parts, w_bo, jb, xf)
        xf = _conv_ffn(xf, ffn_g, w_up, ffn_conv_w, conv_b, w_down, i, seq)
        xf = _ple(xf, ple_g, w_pg, pf, w_pp, i, final_norm_w if i == DEPTH - 1 else None)
    return xf.reshape(batch, seq, d)
```

```python
import functools

import jax
import jax.numpy as jnp
from jax import lax
from jax.experimental import pallas as pl
from jax.experimental.pallas import tpu as pltpu

D_MODEL = 2048
DEPTH = 4
N_A_LAYERS = DEPTH // 2
HEAD_DIM = 128
A_HEADS = D_MODEL // HEAD_DIM
A_WIDTH = A_HEADS * HEAD_DIM
DILATED_GROUPS = ((128, 1), (512, 4), (2048, 16))
N_GROUPS = len(DILATED_GROUPS)
B_HEADS_PER_GROUP = 4
B_WIDTH = N_GROUPS * B_HEADS_PER_GROUP * HEAD_DIM
BAND_BLOCK = 128
BAND_BATCH = 8
ROT_DIM = HEAD_DIM // 4
ROPE_THETA = 500000.0
D_FF = 256 * ((8 * D_MODEL // 3 + 255) // 256)
CONV_WIDTH = 3
PLE_DIM = 256
NORM_EPS = 1e-6
ATTN_SCALE = HEAD_DIM ** -0.5
LOG2_E = 1.4426950408889634
STICK_DONE_LOG2 = 151.0

VMEM_LIMIT_BYTES = 56 * 1024 * 1024
NEG_INF = float("-inf")

_NT_DIMS = (((1,), (1,)), ((), ()))


def _compiler_params(semantics):
    return pltpu.CompilerParams(dimension_semantics=semantics, vmem_limit_bytes=VMEM_LIMIT_BYTES)


def _rms_norm_rows(x, g):
    y = x * lax.rsqrt(jnp.mean(x * x, axis=-1, keepdims=True) + NORM_EPS)
    return y * g


def _layer_spec(block, index_map):
    return pl.BlockSpec((None,) + tuple(block), index_map)


def _rotary_table_kernel(pos_ref, invf_ref, cos_ref, slo_ref, shi_ref):
    ang = pos_ref[...].astype(jnp.float32) * invf_ref[...]
    lane = lax.broadcasted_iota(jnp.int32, ang.shape, 1)
    sin = jnp.sin(ang)
    cos_ref[...] = jnp.cos(ang)
    slo_ref[...] = jnp.where(lane < ROT_DIM // 2, -sin, 0.0)
    shi_ref[...] = jnp.where(lane >= ROT_DIM // 2, sin, 0.0)


def _rotary_tables(pos, invf, *, rows=2048):
    t = pos.shape[0]
    table = jax.ShapeDtypeStruct((t, HEAD_DIM), jnp.float32)
    return pl.pallas_call(
        _rotary_table_kernel,
        grid=(t // rows,),
        in_specs=[pl.BlockSpec((rows, 1), lambda i: (i, 0)), pl.BlockSpec((1, HEAD_DIM), lambda i: (0, 0))],
        out_specs=[pl.BlockSpec((rows, HEAD_DIM), lambda i: (i, 0))] * 3,
        out_shape=[table] * 3,
        compiler_params=_compiler_params(("arbitrary",)),
        name="rotary_tables",
    )(pos, invf)


def _rotary_head(xh, cos, sin_lo, sin_hi):
    return (xh * cos
            + pltpu.roll(xh, HEAD_DIM - ROT_DIM // 2, axis=1) * sin_lo
            + pltpu.roll(xh, ROT_DIM // 2, axis=1) * sin_hi)


def _norm_proj_kernel(*refs, scale, scale_tiles, rot_tiles, tn):
    if rot_tiles:
        x_ref, g_ref, w_ref, cos_ref, slo_ref, shi_ref, o_ref, h_ref = refs
    else:
        x_ref, g_ref, w_ref, o_ref, h_ref = refs
    j = pl.program_id(1)

    @pl.when(j == 0)
    def _():
        h_ref[...] = _rms_norm_rows(x_ref[...], g_ref[...]).astype(h_ref.dtype)

    def project(cols=slice(None)):
        acc = jnp.dot(h_ref[...], w_ref[:, cols], preferred_element_type=jnp.float32)
        if scale_tiles:
            acc = acc * jnp.where(j < scale_tiles, scale, 1.0)
        return acc

    if not rot_tiles:
        o_ref[...] = project().astype(o_ref.dtype)
        return

    @pl.when(j < rot_tiles)
    def _():
        chunk = 2 * HEAD_DIM
        accs = [project(slice(c, c + chunk)) for c in range(0, tn, chunk)]
        cos, slo, shi = cos_ref[...], slo_ref[...], shi_ref[...]
        for ci, acc in enumerate(accs):
            for hd in range(chunk // HEAD_DIM):
                sl = slice(hd * HEAD_DIM, (hd + 1) * HEAD_DIM)
                out_cols = slice(ci * chunk + hd * HEAD_DIM, ci * chunk + (hd + 1) * HEAD_DIM)
                o_ref[:, out_cols] = _rotary_head(acc[:, sl], cos, slo, shi).astype(o_ref.dtype)

    @pl.when(j >= rot_tiles)
    def _():
        o_ref[...] = project().astype(o_ref.dtype)


def _norm_proj(x, g, g_layer, w, w_layer, *, name, out_dtype, scale=1.0, scale_tiles=0, rot_tiles=0,
               tables=None, tm=512, tn=512):
    t, d = x.shape
    n = w.shape[2]
    in_specs = [
        pl.BlockSpec((tm, d), lambda i, j: (i, 0)),
        _layer_spec((1, d), lambda i, j: (g_layer, 0, 0)),
        _layer_spec((d, tn), lambda i, j: (w_layer, 0, j)),
    ]
    args = [x, g, w]
    if rot_tiles:
        in_specs += [pl.BlockSpec((tm, HEAD_DIM), lambda i, j: (i, 0))] * 3
        args += list(tables)
    return pl.pallas_call(
        functools.partial(_norm_proj_kernel, scale=scale, scale_tiles=scale_tiles, rot_tiles=rot_tiles, tn=tn),
        grid=(t // tm, n // tn),
        in_specs=in_specs,
        out_specs=pl.BlockSpec((tm, tn), lambda i, j: (i, j)),
        out_shape=jax.ShapeDtypeStruct((t, n), out_dtype),
        scratch_shapes=[pltpu.VMEM((tm, d), jnp.bfloat16)],
        compiler_params=_compiler_params(("arbitrary", "arbitrary")),
        name=name,
    )(*args)


def _stick_kernel(q_ref, k_ref, v_ref, u2_ref, o_ref, acc_ref, carry_ref, *, tq, heads):
    qi = pl.program_id(2)
    row = lax.broadcasted_iota(jnp.int32, (tq, tq), 0)
    col = lax.broadcasted_iota(jnp.int32, (tq, tq), 1)
    causal = col < row

    def windows(start, n_chunks, diag, carries):
        rows = pl.ds(pl.multiple_of(start, tq), n_chunks * tq)
        hcols = [slice(hh * HEAD_DIM, (hh + 1) * HEAD_DIM) for hh in range(heads)]
        chunks = [(hh, c) for hh in range(heads) for c in reversed(range(n_chunks))]
        is_diag = lambda c: diag and c == n_chunks - 1
        zs = [lax.dot_general(q_ref[:, hc], k_ref[rows, hc], _NT_DIMS, preferred_element_type=jnp.float32)
              for hc in hcols]
        zc = {(hh, c): zs[hh][:, c * tq:(c + 1) * tq] for hh, c in chunks}
        split = {}
        for hh, c in chunks:
            z = zc[hh, c]
            sp = jnp.maximum(z, 0.0) + jnp.log2(1.0 + jnp.exp2(-jnp.abs(z)))
            if is_diag(c):
                sp = jnp.where(causal, sp, 0.0)
            hi = sp.astype(jnp.bfloat16)
            lo = (sp - hi.astype(jnp.float32)).astype(jnp.bfloat16)
            split[hh, c] = jnp.concatenate([hi, lo], axis=1)
        csum = {hc: jnp.dot(split[hc], u2_ref[...], preferred_element_type=jnp.float32) for hc in chunks}
        carries = list(carries) if carries is not None else [None] * heads
        parts = {}
        for hh, c in chunks:
            arg = zc[hh, c] - csum[hh, c]
            if carries[hh] is not None:
                arg = arg - carries[hh]
            a = jnp.exp2(arg)
            if is_diag(c):
                a = jnp.where(causal, a, 0.0)
            parts[hh, c] = a.astype(jnp.bfloat16)
            total = csum[hh, c][:, :1]
            carries[hh] = total if carries[hh] is None else carries[hh] + total
        accs = []
        for hh in range(heads):
            a = [parts[hh, c] for c in range(n_chunks)]
            a = a[0] if n_chunks == 1 else jnp.concatenate(a, axis=1)
            accs.append(jnp.dot(a, v_ref[rows, hcols[hh]], preferred_element_type=jnp.float32))
        return accs, carries

    def store_out(accs):
        for hh in range(heads):
            o_ref[:, hh * HEAD_DIM:(hh + 1) * HEAD_DIM] = accs[hh].astype(o_ref.dtype)

    @pl.when(qi == 0)
    def _():
        store_out(windows(0, 1, True, None)[0])

    @pl.when(qi > 0)
    def _():
        accs, carries = windows((qi - 1) * tq, 2, True, None)
        store_out(accs)

        def reach_of(carries):
            return jnp.min(functools.reduce(jnp.minimum, carries))

        def more(state):
            j, reach = state
            return (j >= 0) & (reach < STICK_DONE_LOG2)

        reach0 = reach_of(carries)

        @pl.when(more((qi - 2, reach0)))
        def _():
            for hh in range(heads):
                acc_ref[hh] = accs[hh]
                carry_ref[hh] = carries[hh]

            def step(state):
                j, _ = state
                more_accs, new_carries = windows(j * tq, 1, False, [carry_ref[hh] for hh in range(heads)])
                for hh in range(heads):
                    acc_ref[hh] += more_accs[hh]
                    carry_ref[hh] = new_carries[hh]
                return j - 1, reach_of(new_carries)

            lax.while_loop(more, step, (qi - 2, reach0))
            store_out([acc_ref[hh] for hh in range(heads)])


def _stick_attention(qkv, batch, seq, *, tq=256, heads=8):
    nq = seq // tq
    hw = heads * HEAD_DIM
    groups = A_WIDTH // hw
    tri = jnp.tril(jnp.ones((tq, tq), jnp.bfloat16))
    return pl.pallas_call(
        functools.partial(_stick_kernel, tq=tq, heads=heads),
        grid=(batch, groups, nq),
        in_specs=[
            pl.BlockSpec((tq, hw), lambda b, h, i: (b * nq + i, h)),
            pl.BlockSpec((seq, hw), lambda b, h, i: (b, groups + h)),
            pl.BlockSpec((seq, hw), lambda b, h, i: (b, 2 * groups + h)),
            pl.BlockSpec((2 * tq, tq), lambda b, h, i: (0, 0)),
        ],
        out_specs=pl.BlockSpec((tq, hw), lambda b, h, i: (b * nq + i, h)),
        out_shape=jax.ShapeDtypeStruct((batch * seq, A_WIDTH), jnp.bfloat16),
        scratch_shapes=[pltpu.VMEM((heads, tq, HEAD_DIM), jnp.float32),
                        pltpu.VMEM((heads, tq, 1), jnp.float32)],
        compiler_params=_compiler_params(("arbitrary", "arbitrary", "arbitrary")),
        name="stick_attention",
    )(qkv, qkv, qkv, jnp.concatenate([tri, tri], axis=0))


def _band_kernel(q0, q1, q2, k0, k1, k2, v0, v1, v2, o0, o1, o2, o_scr, l_scr, *, seq):
    q_refs, k_refs, v_refs, o_refs = (q0, q1, q2), (k0, k1, k2), (v0, v1, v2), (o0, o1, o2)
    blk = BAND_BLOCK
    for g, (window, dil) in enumerate(DILATED_GROUPS):
        steps = window // dil
        n_blk = seq // dil // blk
        kw = 2 * blk if n_blk > 1 else blk
        q_ref, k_ref, v_ref = q_refs[g], k_refs[g], v_refs[g]
        qpos = lax.broadcasted_iota(jnp.int32, (blk, kw), 0)
        kpos = lax.broadcasted_iota(jnp.int32, (blk, kw), 1)

        def body(batch_idx, _, g=g, dil=dil, steps=steps, n_blk=n_blk, kw=kw, q_ref=q_ref, k_ref=k_ref,
                 v_ref=v_ref, qpos=qpos, kpos=kpos):
            pairs = []
            for u in range(BAND_BATCH):
                it = batch_idx * BAND_BATCH + u
                r = it // n_blk
                n = it % n_blk
                k_first = jnp.maximum(n - 1, 0) * blk
                q_rows = pl.ds(r + dil * blk * n, blk, stride=dil)
                k_rows = pl.ds(r + dil * k_first, kw, stride=dil)
                pairs.append((n * blk - k_first, q_rows, k_rows))
            scores = [lax.dot_general(q_ref[q_rows, :].astype(jnp.bfloat16),
                                      k_ref[k_rows, :].astype(jnp.bfloat16), _NT_DIMS,
                                      preferred_element_type=jnp.float32)
                      for _, q_rows, k_rows in pairs]
            soft = []
            for (offset, _, _), s in zip(pairs, scores):
                dist = offset + qpos - kpos
                s = jnp.where((dist >= 0) & (dist <= steps), s, NEG_INF)
                m = jnp.max(s, axis=-1, keepdims=True)
                e = jnp.exp(s - m)
                den = jnp.sum(e, axis=-1, keepdims=True)
                soft.append((e.astype(jnp.bfloat16), den, m + jnp.log(den)))
            outs = [jnp.dot(e, v_ref[k_rows, :].astype(jnp.bfloat16), preferred_element_type=jnp.float32)
                    for (e, _, _), (_, _, k_rows) in zip(soft, pairs)]
            for (_, den, lse), (_, q_rows, _), o in zip(soft, pairs, outs):
                o_scr[g, q_rows, :] = o / den
                l_scr[g, q_rows, :] = jnp.broadcast_to(lse, (blk, HEAD_DIM))
            return 0

        lax.fori_loop(0, dil * n_blk // BAND_BATCH, body, 0)

    chunk = 256
    def mix(c, _):
        rows = pl.ds(pl.multiple_of(c * chunk, chunk), chunk)
        ls = [l_scr[g, rows, :] for g in range(N_GROUPS)]
        mx = jnp.maximum(jnp.maximum(ls[0], ls[1]), ls[2])
        es = [jnp.exp(l - mx) for l in ls]
        tot = es[0] + es[1] + es[2]
        for g in range(N_GROUPS):
            o_refs[g][rows, :] = (o_scr[g, rows, :] * (es[g] / tot)).astype(o_refs[g].dtype)
        return 0

    lax.fori_loop(0, seq // chunk, mix, 0)


def _band_attention(q, kv, batch, seq):
    hg = B_HEADS_PER_GROUP
    n_heads = N_GROUPS * hg

    def col(base, g):
        return lambda b, h: (b, base + g * hg + h)

    blockspec = lambda base, g: pl.BlockSpec((seq, HEAD_DIM), col(base, g))
    return pl.pallas_call(
        functools.partial(_band_kernel, seq=seq),
        grid=(batch, hg),
        in_specs=([blockspec(0, g) for g in range(N_GROUPS)]
                  + [blockspec(0, g) for g in range(N_GROUPS)]
                  + [blockspec(n_heads, g) for g in range(N_GROUPS)]),
        out_specs=[pl.BlockSpec((seq, HEAD_DIM), lambda b, h: (b, h))] * N_GROUPS,
        out_shape=[jax.ShapeDtypeStruct((batch * seq, hg * HEAD_DIM), jnp.bfloat16)] * N_GROUPS,
        scratch_shapes=[pltpu.VMEM((N_GROUPS, seq, HEAD_DIM), jnp.float32)] * 2,
        compiler_params=_compiler_params(("arbitrary", "arbitrary")),
        name="band_attention",
    )(q, q, q, kv, kv, kv, kv, kv, kv)


def _proj_residual_kernel(*refs, n_parts):
    a_refs, w_ref, x_ref, o_ref = refs[:n_parts], refs[n_parts], refs[n_parts + 1], refs[n_parts + 2]
    acc = x_ref[...]
    gw = w_ref.shape[0] // n_parts
    for g in range(n_parts):
        acc = acc + jnp.dot(a_refs[g][...], w_ref[g * gw:(g + 1) * gw, :], preferred_element_type=jnp.float32)
    o_ref[...] = acc


def _proj_residual(a_parts, w, layer, x, *, tm=512):
    t, d = x.shape
    kdim = w.shape[1]
    n_parts = len(a_parts)
    return pl.pallas_call(
        functools.partial(_proj_residual_kernel, n_parts=n_parts),
        grid=(t // tm,),
        in_specs=([pl.BlockSpec((tm, kdim // n_parts), lambda i: (i, 0))] * n_parts
                  + [_layer_spec((kdim, d), lambda i: (layer, 0, 0)),
                     pl.BlockSpec((tm, d), lambda i: (i, 0))]),
        out_specs=pl.BlockSpec((tm, d), lambda i: (i, 0)),
        out_shape=jax.ShapeDtypeStruct((t, d), jnp.float32),
        compiler_params=_compiler_params(("arbitrary",)),
        name="proj_residual",
    )(*a_parts, w, x)


def _conv_ffn_kernel(x_ref, g_ref, wg_ref, wv_ref, cwg_ref, cwv_ref, cbg_ref, cbv_ref, wd_ref,
                     o_ref, h_ref, acta_ref, actb_ref, tail_ref, *, tm, nj, tiles_per_seq):
    i = pl.program_id(0)
    j = pl.program_id(1)
    act_refs = (acta_ref, actb_ref)
    rid = lax.broadcasted_iota(jnp.int32, (tm, 1), 0)

    def up(act_ref):
        def conv_half(w_ref, cw_ref, cb_ref, slot):
            u = jnp.dot(h_ref[...], w_ref[...], preferred_element_type=jnp.float32)
            prev = tail_ref[slot, j]
            p1 = prev[7:8, :]
            p2 = prev[6:7, :]
            um1 = jnp.where(rid == 0, p1, pltpu.roll(u, 1, axis=0))
            um2 = jnp.where(rid == 0, p2, jnp.where(rid == 1, p1, pltpu.roll(u, 2, axis=0)))
            tail_ref[slot, j] = u[tm - 8:, :]
            cw = cw_ref[...]
            return cw[0:1, :] * um2 + cw[1:2, :] * um1 + cw[2:3, :] * u + cb_ref[...]

        gate = conv_half(wg_ref, cwg_ref, cbg_ref, 0)
        val = conv_half(wv_ref, cwv_ref, cbv_ref, 1)
        act_ref[...] = (gate * (1.0 / (1.0 + jnp.exp(-gate))) * val).astype(act_ref.dtype)

    def down(act_ref):
        o_ref[...] += jnp.dot(act_ref[...], wd_ref[...], preferred_element_type=jnp.float32)

    @pl.when(j == 0)
    def _():
        x = x_ref[...]
        h_ref[...] = _rms_norm_rows(x, g_ref[...]).astype(h_ref.dtype)
        o_ref[...] = x

    @pl.when((j == 0) & ((i % tiles_per_seq) == 0))
    def _():
        tail_ref[...] = jnp.zeros(tail_ref.shape, tail_ref.dtype)

    @pl.when(j == 0)
    def _():
        up(act_refs[0])

    for parity in range(2):
        @pl.when((j > 0) & (j < nj) & (j % 2 == parity))
        def _(parity=parity):
            up(act_refs[parity])
            down(act_refs[1 - parity])

    @pl.when(j == nj)
    def _():
        down(act_refs[(nj - 1) % 2])


def _conv_ffn(x, g, w_up, conv_w, conv_b, w_down, layer, seq, *, tm=1024, tf=512):
    t, d = x.shape
    nj = D_FF // tf

    def up_tile(j):
        return jnp.minimum(j, nj - 1)

    def down_tile(j):
        return jnp.maximum(j - 1, 0)

    return pl.pallas_call(
        functools.partial(_conv_ffn_kernel, tm=tm, nj=nj, tiles_per_seq=seq // tm),
        grid=(t // tm, nj + 1),
        in_specs=[
            pl.BlockSpec((tm, d), lambda i, j: (i, 0)),
            _layer_spec((1, d), lambda i, j: (layer, 0, 0)),
            _layer_spec((d, tf), lambda i, j: (layer, 0, up_tile(j))),
            _layer_spec((d, tf), lambda i, j: (layer, 0, nj + up_tile(j))),
            _layer_spec((CONV_WIDTH, tf), lambda i, j: (layer, 0, up_tile(j))),
            _layer_spec((CONV_WIDTH, tf), lambda i, j: (layer, 0, nj + up_tile(j))),
            _layer_spec((1, tf), lambda i, j: (layer, 0, up_tile(j))),
            _layer_spec((1, tf), lambda i, j: (layer, 0, nj + up_tile(j))),
            _layer_spec((tf, d), lambda i, j: (layer, down_tile(j), 0)),
        ],
        out_specs=pl.BlockSpec((tm, d), lambda i, j: (i, 0)),
        out_shape=jax.ShapeDtypeStruct((t, d), jnp.float32),
        scratch_shapes=[pltpu.VMEM((tm, d), jnp.bfloat16),
                        pltpu.VMEM((tm, tf), jnp.bfloat16),
                        pltpu.VMEM((tm, tf), jnp.bfloat16),
                        pltpu.VMEM((2, nj, 8, tf), jnp.float32)],
        compiler_params=_compiler_params(("arbitrary", "arbitrary")),
        name="conv_ffn",
    )(x, g, w_up, w_up, conv_w, conv_w, conv_b, conv_b, w_down)


def _ple_kernel(*refs, final_norm):
    if final_norm:
        x_ref, g_ref, wg_ref, p_ref, wp_ref, fg_ref, o_ref = refs
    else:
        x_ref, g_ref, wg_ref, p_ref, wp_ref, o_ref = refs
    half = x_ref.shape[0] // 2
    rows = [slice(r * half, (r + 1) * half) for r in range(2)]
    xs = [x_ref[r, :] for r in rows]
    hs = [_rms_norm_rows(x, g_ref[...]).astype(jnp.bfloat16) for x in xs]
    projs = [jnp.dot(p_ref[r, :].astype(jnp.bfloat16), wp_ref[...], preferred_element_type=jnp.float32)
             for r in rows]
    gates = [jnp.dot(h, wg_ref[...], preferred_element_type=jnp.float32) for h in hs]
    for r, x, gate, proj in zip(rows, xs, gates, projs):
        y = x + (1.0 / (1.0 + jnp.exp(-gate))) * proj
        if final_norm:
            y = _rms_norm_rows(y, fg_ref[...])
        o_ref[r, :] = y


def _ple(x, g, w_gate, p, w_proj, layer, final_g=None, *, tm=512):
    t, d = x.shape
    final_norm = final_g is not None
    in_specs = [
        pl.BlockSpec((tm, d), lambda i: (i, 0)),
        _layer_spec((1, d), lambda i: (layer, 0, 0)),
        _layer_spec((d, d), lambda i: (layer, 0, 0)),
        _layer_spec((tm, PLE_DIM), lambda i: (layer, i, 0)),
        _layer_spec((PLE_DIM, d), lambda i: (layer, 0, 0)),
    ]
    args = [x, g, w_gate, p, w_proj]
    if final_norm:
        in_specs.append(pl.BlockSpec((1, d), lambda i: (0, 0)))
        args.append(final_g.reshape(1, d))
    return pl.pallas_call(
        functools.partial(_ple_kernel, final_norm=final_norm),
        grid=(t // tm,),
        in_specs=in_specs,
        out_specs=pl.BlockSpec((tm, d), lambda i: (i, 0)),
        out_shape=jax.ShapeDtypeStruct((t, d), jnp.float32),
        compiler_params=_compiler_params(("arbitrary",)),
        name="ple",
    )(*args)


def kernel(x, p, positions, attn_norm_w, a_w_qkv, a_w_o, kv_norm_w, b_w_kv, b_w_q, b_w_o, ffn_norm_w,
           ffn_w_up, ffn_conv_w, ffn_conv_b, ffn_w_down, ple_norm_w, ple_w_gate, ple_w_proj, final_norm_w):
    batch, seq, d = x.shape
    t = batch * seq
    bf = lambda w: w.astype(jnp.bfloat16)
    rows = lambda w: w.reshape(w.shape[0], 1, w.shape[1])
    xf = x.reshape(t, d)
    pf = p.reshape(DEPTH, t, PLE_DIM)

    inv_freq = ROPE_THETA ** (-jnp.arange(0, ROT_DIM, 2, dtype=jnp.float32) / ROT_DIM)
    invf = jnp.zeros((1, HEAD_DIM), jnp.float32).at[0, :ROT_DIM].set(jnp.tile(inv_freq, 2))
    tables = _rotary_tables(positions.reshape(t, 1), invf)

    attn_g, ffn_g, ple_g = rows(attn_norm_w), rows(ffn_norm_w), rows(ple_norm_w)
    kv_g = kv_norm_w.reshape(1, 1, d)
    w_qkv, w_ao = bf(a_w_qkv), bf(a_w_o)
    w_kv, w_q, w_bo = bf(b_w_kv)[None], bf(b_w_q), bf(b_w_o)
    w_up, w_down = bf(ffn_w_up), bf(ffn_w_down)
    conv_b = rows(ffn_conv_b)
    w_pg, w_pp = bf(ple_w_gate), bf(ple_w_proj)

    kv = None
    for i in range(DEPTH):
        if i < N_A_LAYERS:
            qkv = _norm_proj(xf, attn_g, i, w_qkv, i, name="qkv_proj", out_dtype=jnp.bfloat16,
                             scale=ATTN_SCALE * LOG2_E, scale_tiles=A_WIDTH // 1024, tm=1024, tn=1024)
            o = _stick_attention(qkv, batch, seq)
            xf = _proj_residual([o], w_ao, i, xf)
        else:
            if i == N_A_LAYERS:
                kv = _norm_proj(xf, kv_g, 0, w_kv, 0, name="kv_proj", out_dtype=jnp.float32,
                                rot_tiles=2, tables=tables, tm=1024, tn=B_WIDTH // 2)
            jb = i - N_A_LAYERS
            q = _norm_proj(xf, attn_g, i, w_q, jb, name="q_proj", out_dtype=jnp.float32,
                           scale=ATTN_SCALE, scale_tiles=1, rot_tiles=1, tables=tables, tn=B_WIDTH)
            o_parts = _band_attention(q, kv, batch, seq)
            xf = _proj_residual(o_parts, w_bo, jb, xf)
        xf = _conv_ffn(xf, ffn_g, w_up, ffn_conv_w, conv_b, w_down, i, seq)
        xf = _ple(xf, ple_g, w_pg, pf, w_pp, i, final_norm_w if i == DEPTH - 1 else None)
    return xf.reshape(batch, seq, d)
```

```python
import functools

import jax
import jax.numpy as jnp
from jax import lax
from jax.experimental import pallas as pl
from jax.experimental.pallas import tpu as pltpu

D_MODEL = 2048
DEPTH = 4
N_A_LAYERS = DEPTH // 2
HEAD_DIM = 128
A_HEADS = D_MODEL // HEAD_DIM
A_WIDTH = A_HEADS * HEAD_DIM
DILATED_GROUPS = ((128, 1), (512, 4), (2048, 16))
N_GROUPS = len(DILATED_GROUPS)
B_HEADS_PER_GROUP = 4
B_WIDTH = N_GROUPS * B_HEADS_PER_GROUP * HEAD_DIM
BAND_BLOCK = 128
BAND_BATCH = 8
ROT_DIM = HEAD_DIM // 4
ROPE_THETA = 500000.0
D_FF = 256 * ((8 * D_MODEL // 3 + 255) // 256)
CONV_WIDTH = 3
PLE_DIM = 256
NORM_EPS = 1e-6
ATTN_SCALE = HEAD_DIM ** -0.5
LOG2_E = 1.4426950408889634
STICK_DONE_LOG2 = 151.0

VMEM_LIMIT_BYTES = 56 * 1024 * 1024
NEG_INF = float("-inf")

_NT_DIMS = (((1,), (1,)), ((), ()))


def _compiler_params(semantics):
    return pltpu.CompilerParams(dimension_semantics=semantics, vmem_limit_bytes=VMEM_LIMIT_BYTES)


def _rms_norm_rows(x, g):
    y = x * lax.rsqrt(jnp.mean(x * x, axis=-1, keepdims=True) + NORM_EPS)
    return y * g


def _layer_spec(block, index_map):
    return pl.BlockSpec((None,) + tuple(block), index_map)


def _rotary_table_kernel(pos_ref, invf_ref, cos_ref, slo_ref, shi_ref):
    ang = pos_ref[...].astype(jnp.float32) * invf_ref[...]
    lane = lax.broadcasted_iota(jnp.int32, ang.shape, 1)
    sin = jnp.sin(ang)
    cos_ref[...] = jnp.cos(ang)
    slo_ref[...] = jnp.where(lane < ROT_DIM // 2, -sin, 0.0)
    shi_ref[...] = jnp.where(lane >= ROT_DIM // 2, sin, 0.0)


def _rotary_tables(pos, invf, *, rows=2048):
    t = pos.shape[0]
    table = jax.ShapeDtypeStruct((t, HEAD_DIM), jnp.float32)
    return pl.pallas_call(
        _rotary_table_kernel,
        grid=(t // rows,),
        in_specs=[pl.BlockSpec((rows, 1), lambda i: (i, 0)), pl.BlockSpec((1, HEAD_DIM), lambda i: (0, 0))],
        out_specs=[pl.BlockSpec((rows, HEAD_DIM), lambda i: (i, 0))] * 3,
        out_shape=[table] * 3,
        compiler_params=_compiler_params(("arbitrary",)),
        name="rotary_tables",
    )(pos, invf)


def _rotary_head(xh, cos, sin_lo, sin_hi):
    return (xh * cos
            + pltpu.roll(xh, HEAD_DIM - ROT_DIM // 2, axis=1) * sin_lo
            + pltpu.roll(xh, ROT_DIM // 2, axis=1) * sin_hi)


def _norm_proj_kernel(*refs, scale, scale_tiles, rot_tiles, tn):
    if rot_tiles:
        x_ref, g_ref, w_ref, cos_ref, slo_ref, shi_ref, o_ref, h_ref = refs
    else:
        x_ref, g_ref, w_ref, o_ref, h_ref = refs
    j = pl.program_id(1)

    @pl.when(j == 0)
    def _():
        h_ref[...] = _rms_norm_rows(x_ref[...], g_ref[...]).astype(h_ref.dtype)

    def project(cols=slice(None)):
        acc = jnp.dot(h_ref[...], w_ref[:, cols].astype(jnp.bfloat16), preferred_element_type=jnp.float32)
        if scale_tiles:
            acc = acc * jnp.where(j < scale_tiles, scale, 1.0)
        return acc

    if not rot_tiles:
        o_ref[...] = project().astype(o_ref.dtype)
        return

    @pl.when(j < rot_tiles)
    def _():
        chunk = 2 * HEAD_DIM
        accs = [project(slice(c, c + chunk)) for c in range(0, tn, chunk)]
        cos, slo, shi = cos_ref[...], slo_ref[...], shi_ref[...]
        for ci, acc in enumerate(accs):
            for hd in range(chunk // HEAD_DIM):
                sl = slice(hd * HEAD_DIM, (hd + 1) * HEAD_DIM)
                out_cols = slice(ci * chunk + hd * HEAD_DIM, ci * chunk + (hd + 1) * HEAD_DIM)
                o_ref[:, out_cols] = _rotary_head(acc[:, sl], cos, slo, shi).astype(o_ref.dtype)

    @pl.when(j >= rot_tiles)
    def _():
        o_ref[...] = project().astype(o_ref.dtype)


def _norm_proj(x, g, g_layer, w, w_layer, *, name, out_dtype, scale=1.0, scale_tiles=0, rot_tiles=0,
               tables=None, tm=512, tn=512):
    t, d = x.shape
    n = w.shape[2]
    in_specs = [
        pl.BlockSpec((tm, d), lambda i, j: (i, 0)),
        _layer_spec((1, d), lambda i, j: (g_layer, 0, 0)),
        _layer_spec((d, tn), lambda i, j: (w_layer, 0, j)),
    ]
    args = [x, g, w]
    if rot_tiles:
        in_specs += [pl.BlockSpec((tm, HEAD_DIM), lambda i, j: (i, 0))] * 3
        args += list(tables)
    return pl.pallas_call(
        functools.partial(_norm_proj_kernel, scale=scale, scale_tiles=scale_tiles, rot_tiles=rot_tiles, tn=tn),
        grid=(t // tm, n // tn),
        in_specs=in_specs,
        out_specs=pl.BlockSpec((tm, tn), lambda i, j: (i, j)),
        out_shape=jax.ShapeDtypeStruct((t, n), out_dtype),
        scratch_shapes=[pltpu.VMEM((tm, d), jnp.bfloat16)],
        compiler_params=_compiler_params(("arbitrary", "arbitrary")),
        name=name,
    )(*args)


def _stick_kernel(q_ref, k_ref, v_ref, u2_ref, o_ref, acc_ref, carry_ref, *, tq, heads):
    qi = pl.program_id(2)
    row = lax.broadcasted_iota(jnp.int32, (tq, tq), 0)
    col = lax.broadcasted_iota(jnp.int32, (tq, tq), 1)
    causal = col < row

    def windows(start, n_chunks, diag, carries):
        rows = pl.ds(pl.multiple_of(start, tq), n_chunks * tq)
        hcols = [slice(hh * HEAD_DIM, (hh + 1) * HEAD_DIM) for hh in range(heads)]
        chunks = [(hh, c) for hh in range(heads) for c in reversed(range(n_chunks))]
        is_diag = lambda c: diag and c == n_chunks - 1
        zs = [lax.dot_general(q_ref[:, hc], k_ref[rows, hc], _NT_DIMS, preferred_element_type=jnp.float32)
              for hc in hcols]
        zc = {(hh, c): zs[hh][:, c * tq:(c + 1) * tq] for hh, c in chunks}
        split = {}
        for hh, c in chunks:
            z = zc[hh, c]
            sp = jnp.maximum(z, 0.0) + jnp.log2(1.0 + jnp.exp2(-jnp.abs(z)))
            if is_diag(c):
                sp = jnp.where(causal, sp, 0.0)
            hi = sp.astype(jnp.bfloat16)
            lo = (sp - hi.astype(jnp.float32)).astype(jnp.bfloat16)
            split[hh, c] = jnp.concatenate([hi, lo], axis=1)
        csum = {hc: jnp.dot(split[hc], u2_ref[...], preferred_element_type=jnp.float32) for hc in chunks}
        carries = list(carries) if carries is not None else [None] * heads
        parts = {}
        for hh, c in chunks:
            arg = zc[hh, c] - csum[hh, c]
            if carries[hh] is not None:
                arg = arg - carries[hh]
            a = jnp.exp2(arg)
            if is_diag(c):
                a = jnp.where(causal, a, 0.0)
            parts[hh, c] = a.astype(jnp.bfloat16)
            total = csum[hh, c][:, :1]
            carries[hh] = total if carries[hh] is None else carries[hh] + total
        accs = []
        for hh in range(heads):
            a = [parts[hh, c] for c in range(n_chunks)]
            a = a[0] if n_chunks == 1 else jnp.concatenate(a, axis=1)
            accs.append(jnp.dot(a, v_ref[rows, hcols[hh]], preferred_element_type=jnp.float32))
        return accs, carries

    def store_out(accs):
        for hh in range(heads):
            o_ref[:, hh * HEAD_DIM:(hh + 1) * HEAD_DIM] = accs[hh].astype(o_ref.dtype)

    @pl.when(qi == 0)
    def _():
        store_out(windows(0, 1, True, None)[0])

    @pl.when(qi > 0)
    def _():
        accs, carries = windows((qi - 1) * tq, 2, True, None)
        store_out(accs)

        def reach_of(carries):
            return jnp.min(functools.reduce(jnp.minimum, carries))

        def more(state):
            j, reach = state
            return (j >= 0) & (reach < STICK_DONE_LOG2)

        reach0 = reach_of(carries)

        @pl.when(more((qi - 2, reach0)))
        def _():
            for hh in range(heads):
                acc_ref[hh] = accs[hh]
                carry_ref[hh] = carries[hh]

            def step(state):
                j, _ = state
                more_accs, new_carries = windows(j * tq, 1, False, [carry_ref[hh] for hh in range(heads)])
                for hh in range(heads):
                    acc_ref[hh] += more_accs[hh]
                    carry_ref[hh] = new_carries[hh]
                return j - 1, reach_of(new_carries)

            lax.while_loop(more, step, (qi - 2, reach0))
            store_out([acc_ref[hh] for hh in range(heads)])


def _stick_attention(qkv, batch, seq, *, tq=256, heads=8):
    nq = seq // tq
    hw = heads * HEAD_DIM
    groups = A_WIDTH // hw
    tri = jnp.tril(jnp.ones((tq, tq), jnp.bfloat16))
    return pl.pallas_call(
        functools.partial(_stick_kernel, tq=tq, heads=heads),
        grid=(batch, groups, nq),
        in_specs=[
            pl.BlockSpec((tq, hw), lambda b, h, i: (b * nq + i, h)),
            pl.BlockSpec((seq, hw), lambda b, h, i: (b, groups + h)),
            pl.BlockSpec((seq, hw), lambda b, h, i: (b, 2 * groups + h)),
            pl.BlockSpec((2 * tq, tq), lambda b, h, i: (0, 0)),
        ],
        out_specs=pl.BlockSpec((tq, hw), lambda b, h, i: (b * nq + i, h)),
        out_shape=jax.ShapeDtypeStruct((batch * seq, A_WIDTH), jnp.bfloat16),
        scratch_shapes=[pltpu.VMEM((heads, tq, HEAD_DIM), jnp.float32),
                        pltpu.VMEM((heads, tq, 1), jnp.float32)],
        compiler_params=_compiler_params(("arbitrary", "arbitrary", "arbitrary")),
        name="stick_attention",
    )(qkv, qkv, qkv, jnp.concatenate([tri, tri], axis=0))


def _band_kernel(q0, q1, q2, k0, k1, k2, v0, v1, v2, o0, o1, o2, o_scr, l_scr, *, seq):
    q_refs, k_refs, v_refs, o_refs = (q0, q1, q2), (k0, k1, k2), (v0, v1, v2), (o0, o1, o2)
    blk = BAND_BLOCK
    for g, (window, dil) in enumerate(DILATED_GROUPS):
        steps = window // dil
        n_blk = seq // dil // blk
        kw = 2 * blk if n_blk > 1 else blk
        q_ref, k_ref, v_ref = q_refs[g], k_refs[g], v_refs[g]
        qpos = lax.broadcasted_iota(jnp.int32, (blk, kw), 0)
        kpos = lax.broadcasted_iota(jnp.int32, (blk, kw), 1)

        def body(batch_idx, _, g=g, dil=dil, steps=steps, n_blk=n_blk, kw=kw, q_ref=q_ref, k_ref=k_ref,
                 v_ref=v_ref, qpos=qpos, kpos=kpos):
            pairs = []
            for u in range(BAND_BATCH):
                it = batch_idx * BAND_BATCH + u
                r = it // n_blk
                n = it % n_blk
                k_first = jnp.maximum(n - 1, 0) * blk
                q_rows = pl.ds(r + dil * blk * n, blk, stride=dil)
                k_rows = pl.ds(r + dil * k_first, kw, stride=dil)
                pairs.append((n * blk - k_first, q_rows, k_rows))
            scores = [lax.dot_general(q_ref[q_rows, :].astype(jnp.bfloat16),
                                      k_ref[k_rows, :].astype(jnp.bfloat16), _NT_DIMS,
                                      preferred_element_type=jnp.float32)
                      for _, q_rows, k_rows in pairs]
            soft = []
            for (offset, _, _), s in zip(pairs, scores):
                dist = offset + qpos - kpos
                s = jnp.where((dist >= 0) & (dist <= steps), s, NEG_INF)
                m = jnp.max(s, axis=-1, keepdims=True)
                e = jnp.exp(s - m)
                den = jnp.sum(e, axis=-1, keepdims=True)
                soft.append((e.astype(jnp.bfloat16), den, m + jnp.log(den)))
            outs = [jnp.dot(e, v_ref[k_rows, :].astype(jnp.bfloat16), preferred_element_type=jnp.float32)
                    for (e, _, _), (_, _, k_rows) in zip(soft, pairs)]
            for (_, den, lse), (_, q_rows, _), o in zip(soft, pairs, outs):
                o_scr[g, q_rows, :] = o / den
                l_scr[g, q_rows, :] = jnp.broadcast_to(lse, (blk, HEAD_DIM))
            return 0

        lax.fori_loop(0, dil * n_blk // BAND_BATCH, body, 0)

    chunk = 256
    def mix(c, _):
        rows = pl.ds(pl.multiple_of(c * chunk, chunk), chunk)
        ls = [l_scr[g, rows, :] for g in range(N_GROUPS)]
        mx = jnp.maximum(jnp.maximum(ls[0], ls[1]), ls[2])
        es = [jnp.exp(l - mx) for l in ls]
        tot = es[0] + es[1] + es[2]
        for g in range(N_GROUPS):
            o_refs[g][rows, :] = (o_scr[g, rows, :] * (es[g] / tot)).astype(o_refs[g].dtype)
        return 0

    lax.fori_loop(0, seq // chunk, mix, 0)


def _band_attention(q, kv, batch, seq):
    hg = B_HEADS_PER_GROUP
    n_heads = N_GROUPS * hg

    def col(base, g):
        return lambda b, h: (b, base + g * hg + h)

    blockspec = lambda base, g: pl.BlockSpec((seq, HEAD_DIM), col(base, g))
    return pl.pallas_call(
        functools.partial(_band_kernel, seq=seq),
        grid=(batch, hg),
        in_specs=([blockspec(0, g) for g in range(N_GROUPS)]
                  + [blockspec(0, g) for g in range(N_GROUPS)]
                  + [blockspec(n_heads, g) for g in range(N_GROUPS)]),
        out_specs=[pl.BlockSpec((seq, HEAD_DIM), lambda b, h: (b, h))] * N_GROUPS,
        out_shape=[jax.ShapeDtypeStruct((batch * seq, hg * HEAD_DIM), jnp.bfloat16)] * N_GROUPS,
        scratch_shapes=[pltpu.VMEM((N_GROUPS, seq, HEAD_DIM), jnp.float32)] * 2,
        compiler_params=_compiler_params(("arbitrary", "arbitrary")),
        name="band_attention",
    )(q, q, q, kv, kv, kv, kv, kv, kv)


def _proj_residual_kernel(*refs, n_parts):
    a_refs, w_ref, x_ref, o_ref = refs[:n_parts], refs[n_parts], refs[n_parts + 1], refs[n_parts + 2]
    acc = x_ref[...]
    gw = w_ref.shape[0] // n_parts
    for g in range(n_parts):
        acc = acc + jnp.dot(a_refs[g][...], w_ref[g * gw:(g + 1) * gw, :], preferred_element_type=jnp.float32)
    o_ref[...] = acc


def _proj_residual(a_parts, w, layer, x, *, tm=512):
    t, d = x.shape
    kdim = w.shape[1]
    n_parts = len(a_parts)
    return pl.pallas_call(
        functools.partial(_proj_residual_kernel, n_parts=n_parts),
        grid=(t // tm,),
        in_specs=([pl.BlockSpec((tm, kdim // n_parts), lambda i: (i, 0))] * n_parts
                  + [_layer_spec((kdim, d), lambda i: (layer, 0, 0)),
                     pl.BlockSpec((tm, d), lambda i: (i, 0))]),
        out_specs=pl.BlockSpec((tm, d), lambda i: (i, 0)),
        out_shape=jax.ShapeDtypeStruct((t, d), jnp.float32),
        compiler_params=_compiler_params(("arbitrary",)),
        name="proj_residual",
    )(*a_parts, w, x)


def _conv_ffn_kernel(x_ref, g_ref, wg_ref, wv_ref, cwg_ref, cwv_ref, cbg_ref, cbv_ref, wd_ref,
                     o_ref, h_ref, acta_ref, actb_ref, tail_ref, *, tm, nj, tiles_per_seq):
    i = pl.program_id(0)
    j = pl.program_id(1)
    act_refs = (acta_ref, actb_ref)
    rid = lax.broadcasted_iota(jnp.int32, (tm, 1), 0)

    def up(act_ref):
        def conv_half(w_ref, cw_ref, cb_ref, slot):
            u = jnp.dot(h_ref[...], w_ref[...], preferred_element_type=jnp.float32)
            prev = tail_ref[slot, j]
            p1 = prev[7:8, :]
            p2 = prev[6:7, :]
            um1 = jnp.where(rid == 0, p1, pltpu.roll(u, 1, axis=0))
            um2 = jnp.where(rid == 0, p2, jnp.where(rid == 1, p1, pltpu.roll(u, 2, axis=0)))
            tail_ref[slot, j] = u[tm - 8:, :]
            cw = cw_ref[...]
            return cw[0:1, :] * um2 + cw[1:2, :] * um1 + cw[2:3, :] * u + cb_ref[...]

        gate = conv_half(wg_ref, cwg_ref, cbg_ref, 0)
        val = conv_half(wv_ref, cwv_ref, cbv_ref, 1)
        act_ref[...] = (gate * (1.0 / (1.0 + jnp.exp(-gate))) * val).astype(act_ref.dtype)

    def down(act_ref):
        o_ref[...] += jnp.dot(act_ref[...], wd_ref[...], preferred_element_type=jnp.float32)

    @pl.when(j == 0)
    def _():
        x = x_ref[...]
        h_ref[...] = _rms_norm_rows(x, g_ref[...]).astype(h_ref.dtype)
        o_ref[...] = x

    @pl.when((j == 0) & ((i % tiles_per_seq) == 0))
    def _():
        tail_ref[...] = jnp.zeros(tail_ref.shape, tail_ref.dtype)

    @pl.when(j == 0)
    def _():
        up(act_refs[0])

    for parity in range(2):
        @pl.when((j > 0) & (j < nj) & (j % 2 == parity))
        def _(parity=parity):
            up(act_refs[parity])
            down(act_refs[1 - parity])

    @pl.when(j == nj)
    def _():
        down(act_refs[(nj - 1) % 2])


def _conv_ffn(x, g, w_up, conv_w, conv_b, w_down, layer, seq, *, tm=1024, tf=512):
    t, d = x.shape
    nj = D_FF // tf

    def up_tile(j):
        return jnp.minimum(j, nj - 1)

    def down_tile(j):
        return jnp.maximum(j - 1, 0)

    return pl.pallas_call(
        functools.partial(_conv_ffn_kernel, tm=tm, nj=nj, tiles_per_seq=seq // tm),
        grid=(t // tm, nj + 1),
        in_specs=[
            pl.BlockSpec((tm, d), lambda i, j: (i, 0)),
            _layer_spec((1, d), lambda i, j: (layer, 0, 0)),
            _layer_spec((d, tf), lambda i, j: (layer, 0, up_tile(j))),
            _layer_spec((d, tf), lambda i, j: (layer, 0, nj + up_tile(j))),
            _layer_spec((CONV_WIDTH, tf), lambda i, j: (layer, 0, up_tile(j))),
            _layer_spec((CONV_WIDTH, tf), lambda i, j: (layer, 0, nj + up_tile(j))),
            _layer_spec((1, tf), lambda i, j: (layer, 0, up_tile(j))),
            _layer_spec((1, tf), lambda i, j: (layer, 0, nj + up_tile(j))),
            _layer_spec((tf, d), lambda i, j: (layer, down_tile(j), 0)),
        ],
        out_specs=pl.BlockSpec((tm, d), lambda i, j: (i, 0)),
        out_shape=jax.ShapeDtypeStruct((t, d), jnp.float32),
        scratch_shapes=[pltpu.VMEM((tm, d), jnp.bfloat16),
                        pltpu.VMEM((tm, tf), jnp.bfloat16),
                        pltpu.VMEM((tm, tf), jnp.bfloat16),
                        pltpu.VMEM((2, nj, 8, tf), jnp.float32)],
        compiler_params=_compiler_params(("arbitrary", "arbitrary")),
        name="conv_ffn",
    )(x, g, w_up, w_up, conv_w, conv_w, conv_b, conv_b, w_down)


def _ple_kernel(*refs, final_norm):
    if final_norm:
        x_ref, g_ref, wg_ref, p_ref, wp_ref, fg_ref, o_ref = refs
    else:
        x_ref, g_ref, wg_ref, p_ref, wp_ref, o_ref = refs
    half = x_ref.shape[0] // 2
    rows = [slice(r * half, (r + 1) * half) for r in range(2)]
    xs = [x_ref[r, :] for r in rows]
    hs = [_rms_norm_rows(x, g_ref[...]).astype(jnp.bfloat16) for x in xs]
    projs = [jnp.dot(p_ref[r, :].astype(jnp.bfloat16), wp_ref[...], preferred_element_type=jnp.float32)
             for r in rows]
    gates = [jnp.dot(h, wg_ref[...], preferred_element_type=jnp.float32) for h in hs]
    for r, x, gate, proj in zip(rows, xs, gates, projs):
        y = x + (1.0 / (1.0 + jnp.exp(-gate))) * proj
        if final_norm:
            y = _rms_norm_rows(y, fg_ref[...])
        o_ref[r, :] = y


def _ple(x, g, w_gate, p, w_proj, layer, final_g=None, *, tm=512):
    t, d = x.shape
    final_norm = final_g is not None
    in_specs = [
        pl.BlockSpec((tm, d), lambda i: (i, 0)),
        _layer_spec((1, d), lambda i: (layer, 0, 0)),
        _layer_spec((d, d), lambda i: (layer, 0, 0)),
        _layer_spec((tm, PLE_DIM), lambda i: (layer, i, 0)),
        _layer_spec((PLE_DIM, d), lambda i: (layer, 0, 0)),
    ]
    args = [x, g, w_gate, p, w_proj]
    if final_norm:
        in_specs.append(pl.BlockSpec((1, d), lambda i: (0, 0)))
        args.append(final_g.reshape(1, d))
    return pl.pallas_call(
        functools.partial(_ple_kernel, final_norm=final_norm),
        grid=(t // tm,),
        in_specs=in_specs,
        out_specs=pl.BlockSpec((tm, d), lambda i: (i, 0)),
        out_shape=jax.ShapeDtypeStruct((t, d), jnp.float32),
        compiler_params=_compiler_params(("arbitrary",)),
        name="ple",
    )(*args)


def kernel(x, p, positions, attn_norm_w, a_w_qkv, a_w_o, kv_norm_w, b_w_kv, b_w_q, b_w_o, ffn_norm_w,
           ffn_w_up, ffn_conv_w, ffn_conv_b, ffn_w_down, ple_norm_w, ple_w_gate, ple_w_proj, final_norm_w):
    batch, seq, d = x.shape
    t = batch * seq
    bf = lambda w: w.astype(jnp.bfloat16)
    rows = lambda w: w.reshape(w.shape[0], 1, w.shape[1])
    xf = x.reshape(t, d)
    pf = p.reshape(DEPTH, t, PLE_DIM)

    inv_freq = ROPE_THETA ** (-jnp.arange(0, ROT_DIM, 2, dtype=jnp.float32) / ROT_DIM)
    invf = jnp.zeros((1, HEAD_DIM), jnp.float32).at[0, :ROT_DIM].set(jnp.tile(inv_freq, 2))
    tables = _rotary_tables(positions.reshape(t, 1), invf)

    attn_g, ffn_g, ple_g = rows(attn_norm_w), rows(ffn_norm_w), rows(ple_norm_w)
    kv_g = kv_norm_w.reshape(1, 1, d)
    w_qkv, w_ao = a_w_qkv, bf(a_w_o)
    w_kv, w_q, w_bo = b_w_kv[None], bf(b_w_q), bf(b_w_o)
    w_up, w_down = bf(ffn_w_up), bf(ffn_w_down)
    conv_b = rows(ffn_conv_b)
    w_pg, w_pp = bf(ple_w_gate), bf(ple_w_proj)

    kv = None
    for i in range(DEPTH):
        if i < N_A_LAYERS:
            qkv = _norm_proj(xf, attn_g, i, w_qkv, i, name="qkv_proj", out_dtype=jnp.bfloat16,
                             scale=ATTN_SCALE * LOG2_E, scale_tiles=A_WIDTH // 1024, tm=1024, tn=1024)
            o = _stick_attention(qkv, batch, seq)
            xf = _proj_residual([o], w_ao, i, xf)
        else:
            if i == N_A_LAYERS:
                kv = _norm_proj(xf, kv_g, 0, w_kv, 0, name="kv_proj", out_dtype=jnp.float32,
                                rot_tiles=2, tables=tables, tm=1024, tn=B_WIDTH // 2)
            jb = i - N_A_LAYERS
            q = _norm_proj(xf, attn_g, i, w_q, jb, name="q_proj", out_dtype=jnp.float32,
                           scale=ATTN_SCALE, scale_tiles=1, rot_tiles=1, tables=tables, tn=B_WIDTH)
            o_parts = _band_attention(q, kv, batch, seq)
            xf = _proj_residual(o_parts, w_bo, jb, xf)
        xf = _conv_ffn(xf, ffn_g, w_up, ffn_conv_w, conv_b, w_down, i, seq)
        xf = _ple(xf, ple_g, w_pg, pf, w_pp, i, final_norm_w if i == DEPTH - 1 else None)
    return xf.reshape(batch, seq, d)
```
